```python
import jax, jax.numpy as jnp
from jax import lax
import numpy as np

D_MODEL = 1024
BATCH = 32
SEQ = 2048
DEPTH = 4

GRID_W = 64
D_ATTN = D_MODEL // 2
HEAD_DIM = 64
N_HEADS = D_ATTN // HEAD_DIM
WIN_H_MAX = 8
WIN_W = 16
Q_COLS = 16
K_COLS = Q_COLS + WIN_W
NEG_INF = -1e30
D_POOL = D_MODEL - D_ATTN
POOL_WINDOWS = (2, 4, 8, 16)
N_POOL_GROUPS = len(POOL_WINDOWS)
POOL_GROUP_DIM = D_POOL // N_POOL_GROUPS
D_MIX = D_ATTN + D_POOL
D_IN = 3 * D_ATTN + D_POOL
N_GROUPS = 4
EXPERTS_PER_GROUP = 4
N_EXPERTS = N_GROUPS * EXPERTS_PER_GROUP
TOP_K_IN_GROUP = 2
D_EXPERT = 512
EPS = 1e-6

kernel_name = "hymba_natten_poolformer_hiermoe_encoder"


def rmsnorm(x, g):
    xf = x.astype(jnp.float32)
    y = xf * lax.rsqrt(jnp.mean(xf * xf, axis=-1, keepdims=True) + EPS)
    return (y * g.astype(jnp.float32)).astype(x.dtype)


def neighbourhood_attention(q, k, v, rpb):
    B, S, H, hd = q.shape
    rows = S // GRID_W
    kh = min(WIN_H_MAX, rows)
    n_cb = GRID_W // Q_COLS
    qc = np.arange(GRID_W).reshape(n_cb, Q_COLS)
    kc_start = np.clip(np.arange(n_cb) * Q_COLS - WIN_W // 2, 0, GRID_W - K_COLS)
    kc = kc_start[:, None] + np.arange(K_COLS)[None, :]
    qc_start = np.clip(qc - WIN_W // 2, 0, GRID_W - WIN_W)
    col_valid = (kc[:, None, :] >= qc_start[:, :, None]) & (kc[:, None, :] < qc_start[:, :, None] + WIN_W)
    dc = np.clip(kc[:, None, :] - qc[:, :, None], -(WIN_W - 1), WIN_W - 1) + WIN_W - 1
    r = np.arange(rows)
    r_start = np.clip(r - kh // 2, 0, rows - kh)
    kr = r_start[:, None] + np.arange(kh)[None, :]
    dr = kr - r[:, None] + WIN_H_MAX - 1
    bias = jnp.take(rpb[:, dr, :], dc, axis=-1)
    bias = bias.transpose(1, 0, 3, 4, 2, 5).astype(jnp.float32)
    valid = col_valid[:, :, None, :]

    scale = hd ** -0.5
    q_rows = (q * scale).reshape(B, rows, n_cb, Q_COLS, H, hd).transpose(1, 0, 4, 2, 3, 5)
    k_grid = k.reshape(B, rows, GRID_W, H, hd)
    v_grid = v.reshape(B, rows, GRID_W, H, hd)

    def row_block(args):
        rs, q_r, b_r = args
        k_r = jnp.take(lax.dynamic_slice_in_dim(k_grid, rs, kh, axis=1), kc, axis=2)
        v_r = jnp.take(lax.dynamic_slice_in_dim(v_grid, rs, kh, axis=1), kc, axis=2)
        s = jnp.einsum('bhnqd,binkhd->bhnqik', q_r, k_r).astype(jnp.float32) + b_r
        s = jnp.where(valid, s, NEG_INF)
        p = jax.nn.softmax(s.reshape(s.shape[:4] + (kh * K_COLS,)), axis=-1)
        p = p.reshape(s.shape).astype(v.dtype)
        return jnp.einsum('bhnqik,binkhd->bhnqd', p, v_r)

    out = lax.map(row_block, (jnp.asarray(r_start, jnp.int32), q_rows, bias))
    return out.transpose(1, 0, 3, 4, 2, 5).reshape(B, S, H * hd)


def pooling_mixer(u, pool_w, pool_scale):
    B, S, _ = u.shape
    uf = u.astype(jnp.float32)
    cs = jnp.concatenate([jnp.zeros((B, 1, D_POOL), jnp.float32), jnp.cumsum(uf, axis=1)], axis=1)
    t = np.arange(S)
    outs = []
    for g, w in enumerate(POOL_WINDOWS):
        lo = np.clip(t - w // 2, 0, S)
        hi = np.clip(t - w // 2 + w, 0, S)
        cnt = (hi - lo).astype(np.float32)[None, :, None]
        sl = slice(g * POOL_GROUP_DIM, (g + 1) * POOL_GROUP_DIM)
        outs.append((cs[:, hi, sl] - cs[:, lo, sl]) / cnt - uf[:, :, sl])
    pooled = jnp.stack(outs, axis=2).astype(u.dtype)
    mixed = jnp.einsum('bsgc,gcd->bsgd', pooled, pool_w).reshape(B, S, D_POOL)
    return mixed * pool_scale


def hierarchical_moe(h, w_rg, w_re, w_gate, w_up, w_down):
    B, S, D = h.shape
    T = B * S
    tok = h.reshape(T, D)
    g_logits = (tok @ w_rg).astype(jnp.float32)
    g_prob = jax.nn.softmax(g_logits, axis=-1)
    g_idx = jnp.argmax(g_logits, axis=-1)
    g_w = jnp.take_along_axis(g_prob, g_idx[:, None], axis=-1)
    e_logits = (tok @ w_re).astype(jnp.float32).reshape(T, N_GROUPS, EXPERTS_PER_GROUP)
    e_logits = jnp.take_along_axis(e_logits, g_idx[:, None, None], axis=1)[:, 0]
    top_v, top_i = lax.top_k(e_logits, TOP_K_IN_GROUP)
    top_w = jax.nn.softmax(top_v, axis=-1) * g_w
    expert_id = g_idx[:, None] * EXPERTS_PER_GROUP + top_i
    combine = jnp.sum(jax.nn.one_hot(expert_id, N_EXPERTS, dtype=jnp.float32) * top_w[..., None], axis=1)
    y = jnp.zeros((T, D), jnp.float32)
    for e in range(N_EXPERTS):
        a = jax.nn.silu(tok @ w_gate[e]) * (tok @ w_up[e])
        y = y + combine[:, e:e + 1] * (a @ w_down[e]).astype(jnp.float32)
    return y.astype(h.dtype).reshape(B, S, D)


def setup_inputs(seed: int = 0) -> dict:
    key = jax.random.key(seed)
    ks = jax.random.split(key, 16)
    f32 = jnp.float32
    res_scale = (2 * DEPTH) ** -0.5
    x = jax.random.normal(ks[0], (BATCH, SEQ, D_MODEL), f32)
    norm_mix_g = 1.0 + 0.05 * jax.random.normal(ks[1], (DEPTH, D_MODEL), f32)
    w_in = jax.random.normal(ks[2], (DEPTH, D_MODEL, D_IN), f32) * D_MODEL ** -0.5
    rpb = 0.2 * jax.random.normal(ks[3], (DEPTH, N_HEADS, 2 * WIN_H_MAX - 1, 2 * WIN_W - 1), f32)
    pool_w = jax.random.normal(ks[4], (DEPTH, N_POOL_GROUPS, POOL_GROUP_DIM, POOL_GROUP_DIM), f32) * POOL_GROUP_DIM ** -0.5
    pool_scale = 1.0 + 0.1 * jax.random.normal(ks[5], (DEPTH, D_POOL), f32)
    w_out = jax.random.normal(ks[6], (DEPTH, D_MIX, D_MODEL), f32) * (D_MIX ** -0.5) * res_scale
    norm_ffn_g = 1.0 + 0.05 * jax.random.normal(ks[7], (DEPTH, D_MODEL), f32)
    w_router_group = jax.random.normal(ks[8], (DEPTH, D_MODEL, N_GROUPS), f32) * D_MODEL ** -0.5
    w_router_expert = jax.random.normal(ks[9], (DEPTH, D_MODEL, N_EXPERTS), f32) * D_MODEL ** -0.5
    w_gate = jax.random.normal(ks[10], (DEPTH, N_EXPERTS, D_MODEL, D_EXPERT), f32) * D_MODEL ** -0.5
    w_up = jax.random.normal(ks[11], (DEPTH, N_EXPERTS, D_MODEL, D_EXPERT), f32) * D_MODEL ** -0.5
    w_down = jax.random.normal(ks[12], (DEPTH, N_EXPERTS, D_EXPERT, D_MODEL), f32) * (D_EXPERT ** -0.5) * res_scale
    final_g = 1.0 + 0.05 * jax.random.normal(ks[13], (D_MODEL,), f32)
    return {"x": x, "norm_mix_g": norm_mix_g, "w_in": w_in, "rpb": rpb, "pool_w": pool_w,
            "pool_scale": pool_scale, "w_out": w_out, "norm_ffn_g": norm_ffn_g,
            "w_router_group": w_router_group, "w_router_expert": w_router_expert,
            "w_gate": w_gate, "w_up": w_up, "w_down": w_down, "final_g": final_g}


def reference(x, norm_mix_g, w_in, rpb, pool_w, pool_scale, w_out, norm_ffn_g,
              w_router_group, w_router_expert, w_gate, w_up, w_down, final_g):
    B, S, _ = x.shape
    for l in range(DEPTH):
        h = rmsnorm(x, norm_mix_g[l])
        proj = h @ w_in[l]
        q = proj[..., :D_ATTN].reshape(B, S, N_HEADS, HEAD_DIM)
        k = proj[..., D_ATTN:2 * D_ATTN].reshape(B, S, N_HEADS, HEAD_DIM)
        v = proj[..., 2 * D_ATTN:3 * D_ATTN].reshape(B, S, N_HEADS, HEAD_DIM)
        u = proj[..., 3 * D_ATTN:]
        a = neighbourhood_attention(q, k, v, rpb[l])
        p = pooling_mixer(u, pool_w[l], pool_scale[l])
        x = x + jnp.concatenate([a, p], axis=-1) @ w_out[l]
        x = x + hierarchical_moe(rmsnorm(x, norm_ffn_g[l]), w_router_group[l], w_router_expert[l],
                                 w_gate[l], w_up[l], w_down[l])
    return rmsnorm(x, final_g)
```

```python
import functools

import numpy as np
import jax
import jax.numpy as jnp
from jax import lax
from jax.experimental import pallas as pl
from jax.experimental.pallas import tpu as pltpu

F32 = jnp.float32
BF16 = jnp.bfloat16

D_MODEL = 1024
GRID_W = 64
D_ATTN = 512
HEAD_DIM = 64
N_HEADS = 8
N_PAIRS = N_HEADS // 2
WIN_H = 8
WIN_W = 16
D_POOL = 512
POOL_WINDOWS = (2, 4, 8, 16)
POOL_GROUP_DIM = 128
D_IN = 2048
N_GROUPS = 4
EXPERTS_PER_GROUP = 4
N_EXPERTS = 16
D_EXPERT = 512
N_CLASSES = 24
EPS = 1e-6
NEG_INF = -1e30

LANES = 128
Q_ROWS = 8
Q_COLS = 16
K_ROWS = 16
K_COLS = 32
Q_ROT = 8
ROW_BLOCK_KEY_START = (0, 4, 12, 16)
ROW_BLOCK_TYPE = (0, 1, 1, 2)
POOL_HALO = 16

TM_DENSE = 512
TM_MOE = 256
H_EXT = D_MODEL + LANES
VMEM_LIMIT = 56 * 1024 * 1024

PAIR_LO = np.array([0, 0, 0, 1, 1, 2], np.int32)
PAIR_HI = np.array([1, 2, 3, 2, 3, 3], np.int32)


def _params(n_axes=1):
    return pltpu.CompilerParams(dimension_semantics=("arbitrary",) * n_axes, vmem_limit_bytes=VMEM_LIMIT)


def _in_kernel(has_y, *refs):
    if has_y:
        x_ref, y_ref, g_ref, w_ref, xo_ref, proj_ref = refs
        x = x_ref[...] + y_ref[...]
        xo_ref[...] = x
    else:
        x_ref, g_ref, w_ref, proj_ref = refs
        x = x_ref[...]
    tm = x.shape[0]
    ms = jnp.mean(x * x, axis=-1, keepdims=True)
    h = ((x * lax.rsqrt(ms + EPS)) * g_ref[...]).astype(BF16)
    proj = jnp.dot(h, w_ref[...], preferred_element_type=F32)
    q = (proj[:, :D_ATTN] * (HEAD_DIM ** -0.5)).reshape(tm // GRID_W, GRID_W, D_ATTN)
    q = jnp.concatenate([q[:, Q_ROT:], q[:, :Q_ROT]], axis=1).reshape(tm, D_ATTN)
    proj_ref[:, :D_ATTN] = q.astype(BF16)
    proj_ref[:, D_ATTN:] = proj[:, D_ATTN:].astype(BF16)


def _in_proj(x, y, g, w):
    t = x.shape[0]
    tm = TM_DENSE
    row = lambda i: (i, 0)
    fixed = lambda i: (0, 0)
    in_specs = [pl.BlockSpec((tm, D_MODEL), row)]
    args = [x]
    out_shape = []
    out_specs = []
    if y is not None:
        in_specs.append(pl.BlockSpec((tm, D_MODEL), row))
        args.append(y)
        out_shape.append(jax.ShapeDtypeStruct((t, D_MODEL), F32))
        out_specs.append(pl.BlockSpec((tm, D_MODEL), row))
    in_specs += [pl.BlockSpec((1, D_MODEL), fixed), pl.BlockSpec((D_MODEL, D_IN), fixed)]
    args += [g, w]
    out_shape.append(jax.ShapeDtypeStruct((t, D_IN), BF16))
    out_specs.append(pl.BlockSpec((tm, D_IN), row))
    res = pl.pallas_call(
        functools.partial(_in_kernel, y is not None),
        grid=(t // tm,),
        in_specs=in_specs,
        out_specs=out_specs,
        out_shape=out_shape,
        compiler_params=_params(),
        name="in_proj",
    )(*args)
    if y is None:
        return x, res[0]
    return res[0], res[1]


def _attention_index_tables():
    rows = 32
    dr = np.zeros((3, Q_ROWS, K_ROWS), np.int32)
    rv = np.zeros((3, Q_ROWS, K_ROWS), bool)
    for typ, rb in enumerate((0, 1, 3)):
        for i in range(Q_ROWS):
            r = Q_ROWS * rb + i
            r_start = min(max(r - WIN_H // 2, 0), rows - WIN_H)
            for kr in range(K_ROWS):
                ka = ROW_BLOCK_KEY_START[rb] + kr
                rv[typ, i, kr] = r_start <= ka < r_start + WIN_H
                dr[typ, i, kr] = min(max(ka - r + WIN_H - 1, 0), 2 * WIN_H - 2)
    dc = np.zeros((2, Q_COLS, K_COLS), np.int32)
    cv = np.zeros((2, Q_COLS, K_COLS), bool)
    for mt, m in enumerate((0, 3)):
        for j in range(Q_COLS):
            qc = (Q_COLS * m + j + Q_ROT) % GRID_W
            c_start = min(max(qc - WIN_W // 2, 0), GRID_W - WIN_W)
            for kk in range(K_COLS):
                kc = _key_col(m, kk)
                cv[mt, j, kk] = c_start <= kc < c_start + WIN_W
                dc[mt, j, kk] = min(max(kc - qc, -(WIN_W - 1)), WIN_W - 1) + WIN_W - 1
    return dr, rv, dc, cv


def _key_col_segments(m):
    if m < 3:
        return ((Q_COLS * m, K_COLS),)
    return ((0, K_COLS // 2), (GRID_W - K_COLS // 2, K_COLS // 2))


def _key_col(m, kk):
    for start, n in _key_col_segments(m):
        if kk < n:
            return start + kk
        kk -= n
    raise ValueError(kk)


def _attention_bias(rpb):
    dr, rv, dc, cv = _attention_index_tables()
    nl = rpb.shape[0]
    e = rpb[:, :, :, dc]
    e = jnp.take(e, jnp.asarray(dr), axis=2)
    valid = rv[:, :, :, None, None, None] & cv[None, None, None, :, :, :]
    e = jnp.where(jnp.asarray(valid)[None, None], e, NEG_INF)
    e = e.transpose(0, 1, 2, 5, 3, 6, 4, 7)
    e = e.reshape(nl, N_PAIRS, 2, 3, 2, Q_ROWS * Q_COLS, K_ROWS * K_COLS)
    return e.transpose(0, 1, 3, 4, 2, 5, 6).astype(F32)


def _mix_kernel(q_ref, k_ref, v_ref, u_ref, bias_ref, pw_ref, ps_ref, a_ref, p_ref):
    j = pl.program_id(0)
    seq = q_ref.shape[1]
    nq = Q_ROWS * Q_COLS
    lane = lax.broadcasted_iota(jnp.int32, (nq, LANES), 1)
    head0 = lane < HEAD_DIM

    for rb in range(4):
        typ = ROW_BLOCK_TYPE[rb]
        key_row0 = ROW_BLOCK_KEY_START[rb]
        outs = []
        for m in range(4):
            mt = 0 if m < 3 else 1
            qb = jnp.concatenate(
                [q_ref[0, pl.ds((Q_ROWS * rb + i) * GRID_W + Q_COLS * m, Q_COLS), :] for i in range(Q_ROWS)], axis=0)
            segs = _key_col_segments(m)
            kb = jnp.concatenate(
                [k_ref[0, pl.ds((key_row0 + kr) * GRID_W + c0, n), :] for kr in range(K_ROWS) for c0, n in segs], axis=0)
            vb = jnp.concatenate(
                [v_ref[0, pl.ds((key_row0 + kr) * GRID_W + c0, n), :] for kr in range(K_ROWS) for c0, n in segs], axis=0)
            zero = jnp.zeros_like(qb)
            q2 = jnp.concatenate([jnp.where(head0, qb, zero), jnp.where(head0, zero, qb)], axis=0)
            s = lax.dot_general(q2, kb, (((1,), (1,)), ((), ())), preferred_element_type=F32)
            s = s + bias_ref[0, typ, mt].reshape(2 * nq, K_ROWS * K_COLS)
            mx = jnp.max(s, axis=-1, keepdims=True)
            e = jnp.exp(s - mx)
            den = jnp.sum(e, axis=-1, keepdims=True)
            o2 = jnp.dot(e.astype(BF16), vb, preferred_element_type=F32)
            o = jnp.where(head0, o2[:nq], o2[nq:]) / jnp.where(head0, den[:nq], den[nq:])
            outs.append(o)
        for i in range(Q_ROWS):
            pieces = []
            for m in range(4):
                pieces.append(outs[(m - 1) % 4][Q_COLS * i + Q_ROT:Q_COLS * (i + 1)])
                pieces.append(outs[m][Q_COLS * i:Q_COLS * i + Q_ROT])
            a_ref[0, pl.ds((Q_ROWS * rb + i) * GRID_W, GRID_W), :] = jnp.concatenate(pieces, axis=0).astype(BF16)

    u = u_ref[0].astype(F32)
    pad = jnp.zeros((POOL_HALO, LANES), F32)
    ue = jnp.concatenate([pad, u, pad], axis=0)
    n_ext = seq + 2 * POOL_HALO

    def shifted(x, d):
        return pltpu.roll(x, d % n_ext, axis=0)

    s2 = ue + shifted(ue, 1)
    s4 = shifted(s2, 1) + shifted(s2, -1)
    s8 = shifted(s4, 2) + shifted(s4, -2)
    s16 = shifted(s8, 4) + shifted(s8, -4)
    win = jnp.where(j == 0, s2, jnp.where(j == 1, s4, jnp.where(j == 2, s8, s16)))[POOL_HALO:POOL_HALO + seq]
    half = jnp.left_shift(1, j)
    t = lax.broadcasted_iota(jnp.int32, (seq, 1), 0)
    cnt = jnp.minimum(t + half, seq) - jnp.maximum(t - half, 0)
    pooled = win / cnt.astype(F32) - u
    mixed = jnp.dot(pooled.astype(BF16), pw_ref[0], preferred_element_type=F32) * ps_ref[0]
    p_ref[0] = mixed.astype(BF16)


def _mixer(proj, bias, pool_w, pool_scale, batch, seq):
    proj3 = proj.reshape(batch, seq, D_IN)
    col = lambda base: (lambda j, b: (b, 0, base + j))
    blk = (1, seq, LANES)
    bias_blk = (1,) + bias.shape[1:]
    a, p = pl.pallas_call(
        _mix_kernel,
        grid=(N_PAIRS, batch),
        in_specs=[
            pl.BlockSpec(blk, col(0)),
            pl.BlockSpec(blk, col(N_PAIRS)),
            pl.BlockSpec(blk, col(2 * N_PAIRS)),
            pl.BlockSpec(blk, col(3 * N_PAIRS)),
            pl.BlockSpec(bias_blk, lambda j, b: (j, 0, 0, 0, 0, 0)),
            pl.BlockSpec((1, POOL_GROUP_DIM, POOL_GROUP_DIM), lambda j, b: (j, 0, 0)),
            pl.BlockSpec((1, 1, POOL_GROUP_DIM), lambda j, b: (j, 0, 0)),
        ],
        out_specs=[pl.BlockSpec(blk, lambda j, b: (b, 0, j)), pl.BlockSpec(blk, lambda j, b: (b, 0, j))],
        out_shape=[jax.ShapeDtypeStruct((batch, seq, D_ATTN), BF16), jax.ShapeDtypeStruct((batch, seq, D_POOL), BF16)],
        compiler_params=_params(2),
        name="mixer",
    )(proj3, proj3, proj3, proj3, bias, pool_w, pool_scale)
    return a.reshape(batch * seq, D_ATTN), p.reshape(batch * seq, D_POOL)


def _out_kernel(a_ref, p_ref, x_ref, wa_ref, wp_ref, g_ref, wrh_ref, wrl_ref, x2_ref, h_ref):
    mix = jnp.dot(a_ref[...], wa_ref[...], preferred_element_type=F32)
    mix = mix + jnp.dot(p_ref[...], wp_ref[...], preferred_element_type=F32)
    x2 = x_ref[...] + mix
    x2_ref[...] = x2
    ms = jnp.mean(x2 * x2, axis=-1, keepdims=True)
    h = (x2 * lax.rsqrt(ms + EPS)) * g_ref[...]
    h_ref[:, :D_MODEL] = h

    hh = h.astype(BF16)
    hl = (h - hh.astype(F32)).astype(BF16)
    lg = jnp.dot(hh, wrh_ref[...], preferred_element_type=F32)
    lg = lg + jnp.dot(hh, wrl_ref[...], preferred_element_type=F32)
    lg = lg + jnp.dot(hl, wrh_ref[...], preferred_element_type=F32)

    tm = lg.shape[0]
    lane = lax.broadcasted_iota(jnp.int32, (tm, LANES), 1).astype(F32)
    big = jnp.float32(-3e38)
    no_lane = jnp.float32(LANES)
    is_g = lane < N_GROUPS
    gl = jnp.where(is_g, lg, big)
    gmax = jnp.max(gl, axis=-1, keepdims=True)
    gidx = jnp.min(jnp.where(gl == gmax, lane, no_lane), axis=-1, keepdims=True)
    gsum = jnp.sum(jnp.where(is_g, jnp.exp(lg - gmax), 0.0), axis=-1, keepdims=True)
    gw = 1.0 / gsum
    e_lo = N_GROUPS + EXPERTS_PER_GROUP * gidx
    in_grp = (lane >= e_lo) & (lane < e_lo + EXPERTS_PER_GROUP)
    el = jnp.where(in_grp, lg, big)
    v1 = jnp.max(el, axis=-1, keepdims=True)
    i1 = jnp.min(jnp.where(in_grp & (el == v1), lane, no_lane), axis=-1, keepdims=True)
    rest = in_grp & (lane != i1)
    el2 = jnp.where(rest, lg, big)
    v2 = jnp.max(el2, axis=-1, keepdims=True)
    i2 = jnp.min(jnp.where(rest & (el2 == v2), lane, no_lane), axis=-1, keepdims=True)
    d = jnp.exp(v2 - v1)
    w1 = gw * (1.0 / (1.0 + d))
    w2 = gw * (d / (1.0 + d))
    first_lo = i1 < i2
    lo = jnp.where(first_lo, i1, i2) - e_lo
    hi = jnp.where(first_lo, i2, i1) - e_lo
    w_lo = jnp.where(first_lo, w1, w2)
    w_hi = jnp.where(first_lo, w2, w1)
    pair = 0.5 * (lo * (7.0 - lo)) + (hi - lo - 1.0)
    cls = 6.0 * gidx + pair
    h_ref[:, D_MODEL:] = jnp.where(lane == 0, w_lo, jnp.where(lane == 1, w_hi, jnp.where(lane == 2, cls, 0.0)))


def _out_proj(a, p, x, wa, wp, g, wrh, wrl):
    t = x.shape[0]
    tm = TM_DENSE
    row = lambda i: (i, 0)
    fixed = lambda i: (0, 0)
    return pl.pallas_call(
        _out_kernel,
        grid=(t // tm,),
        in_specs=[
            pl.BlockSpec((tm, D_ATTN), row),
            pl.BlockSpec((tm, D_POOL), row),
            pl.BlockSpec((tm, D_MODEL), row),
            pl.BlockSpec((D_ATTN, D_MODEL), fixed),
            pl.BlockSpec((D_POOL, D_MODEL), fixed),
            pl.BlockSpec((1, D_MODEL), fixed),
            pl.BlockSpec((D_MODEL, LANES), fixed),
            pl.BlockSpec((D_MODEL, LANES), fixed),
        ],
        out_specs=[pl.BlockSpec((tm, D_MODEL), row), pl.BlockSpec((tm, H_EXT), row)],
        out_shape=[jax.ShapeDtypeStruct((t, D_MODEL), F32), jax.ShapeDtypeStruct((t, H_EXT), F32)],
        compiler_params=_params(),
        name="out_proj",
    )(a, p, x, wa, wp, g, wrh, wrl)


def _moe_kernel(tok_ref, qb_ref, nv_ref, elo_ref, ehi_ref,
                h_hbm, wg_lo, wu_lo, wd_lo, wg_hi, wu_hi, wd_hi, y_hbm, xbuf, ybuf, gsem, ssem):
    del elo_ref, ehi_ref
    i = pl.program_id(0)
    nv = nv_ref[i]
    qb = qb_ref[i]
    tm = xbuf.shape[0]

    def gather_copy(j, tok):
        return pltpu.make_async_copy(h_hbm.at[pl.ds(tok, 1)], xbuf.at[pl.ds(j, 1)], gsem)

    def scatter_copy(j, dst):
        return pltpu.make_async_copy(ybuf.at[pl.ds(j, 1)], y_hbm.at[pl.ds(dst, 1)], ssem)

    @pl.when(i == 0)
    def _():
        xbuf[...] = jnp.zeros_like(xbuf)

    @pl.when(nv > 0)
    def _():
        def start_gather(j, c):
            gather_copy(j, tok_ref[qb + j]).start()
            return c

        def wait_gather(j, c):
            gather_copy(j, 0).wait()
            return c

        lax.fori_loop(0, nv, start_gather, 0)
        lax.fori_loop(0, nv, wait_gather, 0)

        xe = xbuf[...]
        x = xe[:, :D_MODEL].astype(BF16)
        w_lo = xe[:, D_MODEL:D_MODEL + 1]
        w_hi = xe[:, D_MODEL + 1:D_MODEL + 2]

        def expert(wg, wu, wd):
            a = jax.nn.silu(jnp.dot(x, wg[0], preferred_element_type=F32)) * jnp.dot(x, wu[0], preferred_element_type=F32)
            return jnp.dot(a.astype(BF16), wd[0], preferred_element_type=F32)

        ybuf[...] = w_lo * expert(wg_lo, wu_lo, wd_lo) + w_hi * expert(wg_hi, wu_hi, wd_hi)

        def start_scatter(j, c):
            scatter_copy(j, tok_ref[qb + j]).start()
            return c

        def wait_scatter(j, c):
            scatter_copy(j, 0).wait()
            return c

        lax.fori_loop(0, nv, start_scatter, 0)
        lax.fori_loop(0, nv, wait_scatter, 0)


def _moe_tables(cls, n_tok, tm, n_tiles):
    skey = jnp.sort(cls * n_tok + jnp.arange(n_tok, dtype=jnp.int32))
    tok_sorted = skey % n_tok
    cstart = jnp.searchsorted(skey, jnp.arange(N_CLASSES + 1, dtype=jnp.int32) * n_tok).astype(jnp.int32)
    counts = cstart[1:] - cstart[:-1]
    tiles_per = (counts + tm - 1) // tm
    tile_end = jnp.cumsum(tiles_per)
    tile_start = tile_end - tiles_per
    n_real = tile_end[-1]
    ti = jnp.minimum(jnp.arange(n_tiles, dtype=jnp.int32), n_real - 1)
    c = jnp.minimum(jnp.searchsorted(tile_end, ti, side="right"), N_CLASSES - 1).astype(jnp.int32)
    off = (ti - tile_start[c]) * tm
    real = jnp.arange(n_tiles, dtype=jnp.int32) < n_real
    qbase = cstart[c] + off
    nvalid = jnp.where(real, jnp.clip(counts[c] - off, 0, tm), 0)
    grp = c // 6
    e_lo = EXPERTS_PER_GROUP * grp + jnp.asarray(PAIR_LO)[c % 6]
    e_hi = EXPERTS_PER_GROUP * grp + jnp.asarray(PAIR_HI)[c % 6]
    as_i32 = lambda v: v.astype(jnp.int32)
    return as_i32(tok_sorted), as_i32(qbase), as_i32(nvalid), as_i32(e_lo), as_i32(e_hi)


def _moe(h_ext, w_gate, w_up, w_down):
    n_tok = h_ext.shape[0]
    tm = TM_MOE
    n_tiles = n_tok // tm + N_CLASSES
    cls = h_ext[:, D_MODEL + 2].astype(jnp.int32)
    tables = _moe_tables(cls, n_tok, tm, n_tiles)
    lo = lambda i, tok, qb, nv, elo, ehi: (elo[i], 0, 0)
    hi = lambda i, tok, qb, nv, elo, ehi: (ehi[i], 0, 0)
    up_blk = (1, D_MODEL, D_EXPERT)
    down_blk = (1, D_EXPERT, D_MODEL)
    grid_spec = pltpu.PrefetchScalarGridSpec(
        num_scalar_prefetch=5,
        grid=(n_tiles,),
        in_specs=[
            pl.BlockSpec(memory_space=pl.ANY),
            pl.BlockSpec(up_blk, lo), pl.BlockSpec(up_blk, lo), pl.BlockSpec(down_blk, lo),
            pl.BlockSpec(up_blk, hi), pl.BlockSpec(up_blk, hi), pl.BlockSpec(down_blk, hi),
        ],
        out_specs=pl.BlockSpec(memory_space=pl.ANY),
        scratch_shapes=[
            pltpu.VMEM((tm, H_EXT), F32),
            pltpu.VMEM((tm, D_MODEL), F32),
            pltpu.SemaphoreType.DMA,
            pltpu.SemaphoreType.DMA,
        ],
    )
    return pl.pallas_call(
        _moe_kernel,
        grid_spec=grid_spec,
        out_shape=jax.ShapeDtypeStruct((n_tok, D_MODEL), F32),
        compiler_params=_params(),
        name="moe",
    )(*tables, h_ext, w_gate, w_up, w_down, w_gate, w_up, w_down)


def _final_kernel(x_ref, y_ref, g_ref, o_ref):
    x = x_ref[...] + y_ref[...]
    ms = jnp.mean(x * x, axis=-1, keepdims=True)
    o_ref[...] = (x * lax.rsqrt(ms + EPS)) * g_ref[...]


def _final_norm(x, y, g):
    t = x.shape[0]
    tm = TM_DENSE
    row = lambda i: (i, 0)
    return pl.pallas_call(
        _final_kernel,
        grid=(t // tm,),
        in_specs=[pl.BlockSpec((tm, D_MODEL), row), pl.BlockSpec((tm, D_MODEL), row),
                  pl.BlockSpec((1, D_MODEL), lambda i: (0, 0))],
        out_specs=pl.BlockSpec((tm, D_MODEL), row),
        out_shape=jax.ShapeDtypeStruct((t, D_MODEL), F32),
        compiler_params=_params(),
        name="final_norm",
    )(x, y, g)


def _split_bf16(w):
    hi = w.astype(BF16)
    lo = (w - hi.astype(F32)).astype(BF16)
    return hi, lo


def kernel(x, norm_mix_g, w_in, rpb, pool_w, pool_scale, w_out, norm_ffn_g,
           w_router_group, w_router_expert, w_gate, w_up, w_down, final_g):
    batch, seq, d = x.shape
    depth = w_in.shape[0]
    assert d == D_MODEL and seq % GRID_W == 0 and seq // GRID_W == 32
    n_tok = batch * seq
    assert n_tok % TM_DENSE == 0 and n_tok % TM_MOE == 0

    bias = _attention_bias(rpb)
    w_in_b = w_in.astype(BF16)
    w_out_b = w_out.astype(BF16)
    pool_w_b = pool_w.astype(BF16)
    pool_scale3 = pool_scale.reshape(depth, N_PAIRS, 1, POOL_GROUP_DIM)
    w_gate_b = w_gate.astype(BF16)
    w_up_b = w_up.astype(BF16)
    w_down_b = w_down.astype(BF16)
    w_router = jnp.concatenate([w_router_group, w_router_expert], axis=-1)
    w_router = jnp.pad(w_router, ((0, 0), (0, 0), (0, LANES - w_router.shape[-1])))
    wr_hi, wr_lo = _split_bf16(w_router)

    xf = x.reshape(n_tok, D_MODEL)
    y = None
    for l in range(depth):
        xf, proj = _in_proj(xf, None if y is None else y, norm_mix_g[l].reshape(1, D_MODEL), w_in_b[l])
        a, p = _mixer(proj, bias[l], pool_w_b[l], pool_scale3[l], batch, seq)
        xf, h_ext = _out_proj(a, p, xf, w_out_b[l, :D_ATTN], w_out_b[l, D_ATTN:], norm_ffn_g[l].reshape(1, D_MODEL),
                              wr_hi[l], wr_lo[l])
        y = _moe(h_ext, w_gate_b[l], w_up_b[l], w_down_b[l])
    out = _final_norm(xf, y, final_g.reshape(1, D_MODEL))
    return out.reshape(batch, seq, D_MODEL)
```

```python
import functools

import numpy as np
import jax
import jax.numpy as jnp
from jax import lax
from jax.experimental import pallas as pl
from jax.experimental.pallas import tpu as pltpu

F32 = jnp.float32
BF16 = jnp.bfloat16
I32 = jnp.int32

D_MODEL = 1024
GRID_W = 64
GRID_H = 32
D_ATTN = 512
HEAD_DIM = 64
N_HEADS = 8
WIN_H = 8
WIN_W = 16
D_POOL = 512
POOL_GROUP_DIM = 128
D_IN = 2048
N_GROUPS = 4
EXPERTS_PER_GROUP = 4
D_EXPERT = 512
N_CLASSES = 24
EPS = 1e-6
NEG_INF = -1e30

LANES = 128
SUBLANES = 8
TOKEN_ROWS = D_MODEL // LANES
COL_BLOCK = 256
HEADS_PER_BLOCK = COL_BLOCK // HEAD_DIM
GROUPS_PER_BLOCK = COL_BLOCK // POOL_GROUP_DIM
N_COL_BLOCKS = D_ATTN // COL_BLOCK
Q_ROWS = 8
Q_COLS = 16
K_ROWS = 16
K_COLS = 32
Q_ROT = 8
ROW_BLOCK_KEY_START = (0, 4, 12, 16)
POOL_HALO = 16
POOL_WINDOWS = (2, 4, 8, 16)

TM_DENSE = 512
TM_MOE = 256
VMEM_LIMIT = 56 * 1024 * 1024

PAIR_LO = np.array([0, 0, 0, 1, 1, 2], np.int32)
PAIR_HI = np.array([1, 2, 3, 2, 3, 3], np.int32)


def _params(n_axes=1):
    return pltpu.CompilerParams(dimension_semantics=("arbitrary",) * n_axes, vmem_limit_bytes=VMEM_LIMIT)


def _rms_scale(x):
    return lax.rsqrt(jnp.mean(x * x, axis=-1, keepdims=True) + EPS)


def _load_token_tiles(ref, n_tok):
    return jnp.concatenate([ref[pl.ds(s, n_tok, stride=TOKEN_ROWS), :] for s in range(TOKEN_ROWS)], axis=1)


def _store_token_tiles(ref, val):
    n_tok = val.shape[0]
    for s in range(TOKEN_ROWS):
        ref[pl.ds(s, n_tok, stride=TOKEN_ROWS), :] = val[:, s * LANES:(s + 1) * LANES]


def _in_kernel(has_y, *refs):
    if has_y:
        x_ref, y_ref, g_ref, w_ref, xo_ref, proj_ref = refs
        x = x_ref[...] + _load_token_tiles(y_ref, x_ref.shape[0])
        xo_ref[...] = x
    else:
        x_ref, g_ref, w_ref, proj_ref = refs
        x = x_ref[...]
    tm = x.shape[0]
    h = ((x * _rms_scale(x)) * g_ref[...]).astype(BF16)
    proj = jnp.dot(h, w_ref[...], preferred_element_type=F32)
    q = (proj[:, :D_ATTN] * (HEAD_DIM ** -0.5)).reshape(tm // GRID_W, GRID_W, D_ATTN)
    q = jnp.concatenate([q[:, Q_ROT:], q[:, :Q_ROT]], axis=1).reshape(tm, D_ATTN)
    proj_ref[:, :D_ATTN] = q.astype(BF16)
    proj_ref[:, D_ATTN:] = proj[:, D_ATTN:].astype(BF16)


def _in_proj(x, y, g, w):
    t = x.shape[0]
    tm = TM_DENSE
    row = lambda i: (i, 0)
    fixed = lambda i: (0, 0)
    in_specs = [pl.BlockSpec((tm, D_MODEL), row)]
    args = [x]
    out_shape = []
    out_specs = []
    if y is not None:
        in_specs.append(pl.BlockSpec((tm * TOKEN_ROWS, LANES), row))
        args.append(y)
        out_shape.append(jax.ShapeDtypeStruct((t, D_MODEL), F32))
        out_specs.append(pl.BlockSpec((tm, D_MODEL), row))
    in_specs += [pl.BlockSpec((1, D_MODEL), fixed), pl.BlockSpec((D_MODEL, D_IN), fixed)]
    args += [g, w]
    out_shape.append(jax.ShapeDtypeStruct((t, D_IN), BF16))
    out_specs.append(pl.BlockSpec((tm, D_IN), row))
    res = pl.pallas_call(
        functools.partial(_in_kernel, y is not None),
        grid=(t // tm,),
        in_specs=in_specs,
        out_specs=out_specs,
        out_shape=out_shape,
        compiler_params=_params(),
        name="in_proj",
    )(*args)
    if y is None:
        return x, res[0]
    return res[0], res[1]


def _key_col_segments(m):
    if m < 3:
        return ((Q_COLS * m, K_COLS),)
    return ((0, K_COLS // 2), (GRID_W - K_COLS // 2, K_COLS // 2))


def _key_col(m, kk):
    for start, n in _key_col_segments(m):
        if kk < n:
            return start + kk
        kk -= n
    raise ValueError(kk)


def _attention_index_tables():
    dr = np.zeros((3, Q_ROWS, K_ROWS), np.int32)
    rv = np.zeros((3, Q_ROWS, K_ROWS), bool)
    for typ, rb in enumerate((0, 1, 3)):
        for i in range(Q_ROWS):
            r = Q_ROWS * rb + i
            r_start = min(max(r - WIN_H // 2, 0), GRID_H - WIN_H)
            for kr in range(K_ROWS):
                ka = ROW_BLOCK_KEY_START[rb] + kr
                rv[typ, i, kr] = r_start <= ka < r_start + WIN_H
                dr[typ, i, kr] = min(max(ka - r + WIN_H - 1, 0), 2 * WIN_H - 2)
    dc = np.zeros((2, Q_COLS, K_COLS), np.int32)
    cv = np.zeros((2, Q_COLS, K_COLS), bool)
    for mt, m in enumerate((0, 3)):
        for j in range(Q_COLS):
            qc = (Q_COLS * m + j + Q_ROT) % GRID_W
            c_start = min(max(qc - WIN_W // 2, 0), GRID_W - WIN_W)
            for kk in range(K_COLS):
                kc = _key_col(m, kk)
                cv[mt, j, kk] = c_start <= kc < c_start + WIN_W
                dc[mt, j, kk] = min(max(kc - qc, -(WIN_W - 1)), WIN_W - 1) + WIN_W - 1
    return dr, rv, dc, cv


def _attention_bias(rpb):
    dr, rv, dc, cv = _attention_index_tables()
    nl = rpb.shape[0]
    e = rpb[:, :, :, dc]
    e = jnp.take(e, jnp.asarray(dr), axis=2)
    valid = rv[:, :, :, None, None, None] & cv[None, None, None, :, :, :]
    e = jnp.where(jnp.asarray(valid)[None, None], e, NEG_INF)
    e = e.transpose(0, 1, 2, 5, 3, 6, 4, 7)
    e = e.reshape(nl, N_COL_BLOCKS, HEADS_PER_BLOCK, 3, 2, Q_ROWS * Q_COLS, K_ROWS * K_COLS)
    return e.transpose(0, 1, 3, 4, 2, 5, 6).astype(F32)


def _pool_counts(seq):
    t = np.arange(seq)
    cols = []
    for w in POOL_WINDOWS:
        cnt = np.minimum(t + w // 2, seq) - np.maximum(t - w // 2, 0)
        cols.append(np.broadcast_to(cnt[:, None].astype(np.float32), (seq, POOL_GROUP_DIM)))
    return np.concatenate(cols, axis=1).reshape(seq, N_COL_BLOCKS, COL_BLOCK).transpose(1, 0, 2)


def _window_sum(ue, window):
    n = ue.shape[0]

    def shifted(x, d):
        return pltpu.roll(x, d % n, axis=0)

    s = ue + shifted(ue, 1)
    reach = 1
    while 2 * reach < window:
        s = shifted(s, reach) + shifted(s, -reach)
        reach *= 2
    return s


def _mix_kernel(q_ref, k_ref, v_ref, u_ref, bias_ref, cnt_ref, pw_ref, ps_ref, a_ref, p_ref):
    jb = pl.program_id(0)
    seq = q_ref.shape[1]
    nq = Q_ROWS * Q_COLS
    nk = K_ROWS * K_COLS
    lane_head = lax.broadcasted_iota(I32, (nq, COL_BLOCK), 1) // HEAD_DIM

    def rows(ref, grid_row, col, n):
        return ref[0, pl.ds(pl.multiple_of(grid_row * GRID_W + col, Q_COLS), n), :]

    def row_block(rb, carry):
        key_row0 = jnp.clip(Q_ROWS * rb - WIN_H // 2, 0, GRID_H - K_ROWS)
        typ = jnp.where(rb == 0, 0, jnp.where(rb == GRID_H // Q_ROWS - 1, 2, 1))
        outs = []
        for m in range(GRID_W // Q_COLS):
            mt = 0 if m < 3 else 1
            qb = jnp.concatenate([rows(q_ref, Q_ROWS * rb + i, Q_COLS * m, Q_COLS) for i in range(Q_ROWS)], axis=0)
            segs = _key_col_segments(m)
            kb = jnp.concatenate(
                [rows(k_ref, key_row0 + kr, c0, n) for kr in range(K_ROWS) for c0, n in segs], axis=0)
            vb = jnp.concatenate(
                [rows(v_ref, key_row0 + kr, c0, n) for kr in range(K_ROWS) for c0, n in segs], axis=0)
            zero = jnp.zeros_like(qb)
            qh = jnp.concatenate([jnp.where(lane_head == h, qb, zero) for h in range(HEADS_PER_BLOCK)], axis=0)
            s = lax.dot_general(qh, kb, (((1,), (1,)), ((), ())), preferred_element_type=F32)
            s = s + bias_ref[0, typ, mt].reshape(HEADS_PER_BLOCK * nq, nk)
            mx = jnp.max(s, axis=-1, keepdims=True)
            e = jnp.exp(s - mx)
            den = jnp.sum(e, axis=-1, keepdims=True)
            oh = jnp.dot(e.astype(BF16), vb, preferred_element_type=F32)
            o = oh[:nq]
            dn = jnp.broadcast_to(den[:nq], (nq, COL_BLOCK))
            for h in range(1, HEADS_PER_BLOCK):
                o = jnp.where(lane_head == h, oh[h * nq:(h + 1) * nq], o)
                dn = jnp.where(lane_head == h, den[h * nq:(h + 1) * nq], dn)
            outs.append(o / dn)
        for i in range(Q_ROWS):
            pieces = []
            for m in range(GRID_W // Q_COLS):
                pieces.append(outs[(m - 1) % 4][Q_COLS * i + Q_ROT:Q_COLS * (i + 1)])
                pieces.append(outs[m][Q_COLS * i:Q_COLS * i + Q_ROT])
            start = pl.multiple_of((Q_ROWS * rb + i) * GRID_W, GRID_W)
            a_ref[0, pl.ds(start, GRID_W), :] = jnp.concatenate(pieces, axis=0).astype(BF16)
        return carry

    lax.fori_loop(0, GRID_H // Q_ROWS, row_block, 0)

    def pool(windows):
        u = u_ref[0].astype(F32)
        pad = jnp.zeros((POOL_HALO, POOL_GROUP_DIM), F32)
        for gi, window in enumerate(windows):
            sl = slice(gi * POOL_GROUP_DIM, (gi + 1) * POOL_GROUP_DIM)
            ug = u[:, sl]
            win = _window_sum(jnp.concatenate([pad, ug, pad], axis=0), window)[POOL_HALO:POOL_HALO + seq]
            pooled = win / cnt_ref[0, :, sl] - ug
            mixed = jnp.dot(pooled.astype(BF16), pw_ref[gi], preferred_element_type=F32) * ps_ref[gi]
            p_ref[0, :, sl] = mixed.astype(BF16)

    for cb in range(N_COL_BLOCKS):
        @pl.when(jb == cb)
        def _(cb=cb):
            pool(POOL_WINDOWS[cb * GROUPS_PER_BLOCK:(cb + 1) * GROUPS_PER_BLOCK])


def _mixer(proj, bias, cnt, pool_w, pool_scale, batch, seq):
    proj3 = proj.reshape(batch, seq, D_IN)
    col = lambda base: (lambda j, b: (b, 0, base + j))
    blk = (1, seq, COL_BLOCK)
    bias_blk = (1,) + bias.shape[1:]
    a, p = pl.pallas_call(
        _mix_kernel,
        grid=(N_COL_BLOCKS, batch),
        in_specs=[
            pl.BlockSpec(blk, col(0)),
            pl.BlockSpec(blk, col(N_COL_BLOCKS)),
            pl.BlockSpec(blk, col(2 * N_COL_BLOCKS)),
            pl.BlockSpec(blk, col(3 * N_COL_BLOCKS)),
            pl.BlockSpec(bias_blk, lambda j, b: (j, 0, 0, 0, 0, 0)),
            pl.BlockSpec(blk, lambda j, b: (j, 0, 0)),
            pl.BlockSpec((GROUPS_PER_BLOCK, POOL_GROUP_DIM, POOL_GROUP_DIM), lambda j, b: (j, 0, 0)),
            pl.BlockSpec((GROUPS_PER_BLOCK, 1, POOL_GROUP_DIM), lambda j, b: (j, 0, 0)),
        ],
        out_specs=[pl.BlockSpec(blk, lambda j, b: (b, 0, j)), pl.BlockSpec(blk, lambda j, b: (b, 0, j))],
        out_shape=[jax.ShapeDtypeStruct((batch, seq, D_ATTN), BF16), jax.ShapeDtypeStruct((batch, seq, D_POOL), BF16)],
        compiler_params=_params(2),
        name="mixer",
    )(proj3, proj3, proj3, proj3, bias, cnt, pool_w, pool_scale)
    return a.reshape(batch * seq, D_ATTN), p.reshape(batch * seq, D_POOL)


ROUTER_ROWS = 32
OUT_SPLIT = 2


def _first_max(vals):
    best = vals[0]
    for v in vals[1:]:
        best = jnp.maximum(best, v)
    idx = jnp.float32(len(vals) - 1)
    for i in range(len(vals) - 2, -1, -1):
        idx = jnp.where(vals[i] == best, jnp.float32(i), idx)
    return best, idx


def _route(lg):
    n = lg.shape[1]
    big = jnp.float32(-3e38)
    g = [lg[i:i + 1] for i in range(N_GROUPS)]
    gmax, gidx = _first_max(g)
    gsum = jnp.exp(g[0] - gmax)
    for v in g[1:]:
        gsum = gsum + jnp.exp(v - gmax)
    gw = 1.0 / gsum
    e = []
    for j in range(EXPERTS_PER_GROUP):
        v = lg[N_GROUPS + j:N_GROUPS + j + 1]
        for gg in range(1, N_GROUPS):
            row = N_GROUPS + EXPERTS_PER_GROUP * gg + j
            v = jnp.where(gidx == gg, lg[row:row + 1], v)
        e.append(v)
    v1, i1 = _first_max(e)
    v2, i2 = _first_max([jnp.where(i1 == j, big, e[j]) for j in range(EXPERTS_PER_GROUP)])
    d = jnp.exp(v2 - v1)
    w1 = gw * (1.0 / (1.0 + d))
    w2 = gw * (d / (1.0 + d))
    first_lo = i1 < i2
    lo = jnp.where(first_lo, i1, i2)
    hi = jnp.where(first_lo, i2, i1)
    w_lo = jnp.where(first_lo, w1, w2)
    w_hi = jnp.where(first_lo, w2, w1)
    cls = 6.0 * gidx + 0.5 * (lo * (7.0 - lo)) + (hi - lo - 1.0)
    return jnp.concatenate([w_lo, w_hi, cls, jnp.zeros((SUBLANES - 3, n), F32)], axis=0)


def _out_kernel(a_ref, p_ref, x_ref, wa_ref, wp_ref, g_ref, wr_ref, x2_ref, h_ref, r_ref):
    tm = x_ref.shape[0]
    n = tm // OUT_SPLIT
    nt_dims = (((1,), (1,)), ((), ()))
    for c in range(OUT_SPLIT):
        sl = pl.ds(c * n, n)
        mix = jnp.dot(a_ref[sl, :], wa_ref[...], preferred_element_type=F32)
        mix = mix + jnp.dot(p_ref[sl, :], wp_ref[...], preferred_element_type=F32)
        x2 = x_ref[sl, :] + mix
        x2_ref[sl, :] = x2
        h = (x2 * _rms_scale(x2)) * g_ref[...]
        _store_token_tiles(h_ref.at[pl.ds(c * n * TOKEN_ROWS, n * TOKEN_ROWS)], h)
        hh = h.astype(BF16)
        hl = (h - hh.astype(F32)).astype(BF16)
        r1 = lax.dot_general(wr_ref[...], hh, nt_dims, preferred_element_type=F32)
        r2 = lax.dot_general(wr_ref[:ROUTER_ROWS], hl, nt_dims, preferred_element_type=F32)
        r_ref[0, :, sl] = _route(r1[:ROUTER_ROWS] + r1[ROUTER_ROWS:] + r2)


def _out_proj(a, p, x, wa, wp, g, wr):
    t = x.shape[0]
    tm = TM_DENSE
    row = lambda i: (i, 0)
    fixed = lambda i: (0, 0)
    return pl.pallas_call(
        _out_kernel,
        grid=(t // tm,),
        in_specs=[
            pl.BlockSpec((tm, D_ATTN), row),
            pl.BlockSpec((tm, D_POOL), row),
            pl.BlockSpec((tm, D_MODEL), row),
            pl.BlockSpec((D_ATTN, D_MODEL), fixed),
            pl.BlockSpec((D_POOL, D_MODEL), fixed),
            pl.BlockSpec((1, D_MODEL), fixed),
            pl.BlockSpec((2 * ROUTER_ROWS, D_MODEL), fixed),
        ],
        out_specs=[pl.BlockSpec((tm, D_MODEL), row),
                   pl.BlockSpec((tm * TOKEN_ROWS, LANES), row),
                   pl.BlockSpec((1, SUBLANES, tm), lambda i: (i, 0, 0))],
        out_shape=[jax.ShapeDtypeStruct((t, D_MODEL), F32),
                   jax.ShapeDtypeStruct((t * TOKEN_ROWS, LANES), F32),
                   jax.ShapeDtypeStruct((t // tm, SUBLANES, tm), F32)],
        compiler_params=_params(),
        name="out_proj",
    )(a, p, x, wa, wp, g, wr)


def _moe_kernel(tok_ref, tile_ref, first_ref, last_ref, lo_ref, hi_ref, elo_ref, ehi_ref,
                h_hbm, w_ref, wg_lo, wu_lo, wd_lo, wg_hi, wu_hi, wd_hi, y_hbm,
                xbuf, ybuf, acc, gsem, ssem):
    del elo_ref, ehi_ref
    k = pl.program_id(0)
    tm = acc.shape[0]
    n_tiles = y_hbm.shape[0] // (tm * TOKEN_ROWS)
    tile = tile_ref[k]
    slot = tile % 2

    def token_rows(idx):
        return pl.ds(pl.multiple_of(idx * TOKEN_ROWS, TOKEN_ROWS), TOKEN_ROWS)

    def gather_copy(j, tok, sl):
        return pltpu.make_async_copy(h_hbm.at[token_rows(tok)], xbuf.at[sl, token_rows(j)], gsem.at[sl])

    def scatter_copy(j, tok, sl):
        return pltpu.make_async_copy(ybuf.at[sl, token_rows(j)], y_hbm.at[token_rows(tok)], ssem.at[sl])

    def start_rows(copy, tl, sl):
        def body(j, c):
            copy(j, tok_ref[tl * tm + j], sl).start()
            return c
        lax.fori_loop(0, tm, body, 0, unroll=8)

    def wait_rows(copy, sl):
        for j in range(tm):
            copy(j, 0, sl).wait()

    @pl.when(first_ref[k] == 1)
    def _():
        @pl.when(k == 0)
        def _():
            start_rows(gather_copy, tile, slot)
        wait_rows(gather_copy, slot)

        @pl.when(tile + 1 < n_tiles)
        def _():
            start_rows(gather_copy, tile + 1, 1 - slot)

        @pl.when(tile >= 2)
        def _():
            wait_rows(scatter_copy, slot)
        acc[...] = jnp.zeros_like(acc)

    lo = lo_ref[k]
    hi = hi_ref[k]

    @pl.when(hi > lo)
    def _():
        x = _load_token_tiles(xbuf.at[slot], tm).astype(BF16)
        row = lax.broadcasted_iota(I32, (tm, 1), 0)
        mine = (row >= lo) & (row < hi)
        w = w_ref[...]
        w_lo = jnp.where(mine, w[:, 0:1], 0.0)
        w_hi = jnp.where(mine, w[:, 1:2], 0.0)

        def expert(wg, wu, wd):
            a = jax.nn.silu(jnp.dot(x, wg[0], preferred_element_type=F32)) * jnp.dot(x, wu[0], preferred_element_type=F32)
            return jnp.dot(a.astype(BF16), wd[0], preferred_element_type=F32)

        acc[...] += w_lo * expert(wg_lo, wu_lo, wd_lo) + w_hi * expert(wg_hi, wu_hi, wd_hi)

    @pl.when(last_ref[k] == 1)
    def _():
        _store_token_tiles(ybuf.at[slot], acc[...])
        start_rows(scatter_copy, tile, slot)

    @pl.when(k == pl.num_programs(0) - 1)
    def _():
        wait_rows(scatter_copy, slot)
        if n_tiles > 1:
            wait_rows(scatter_copy, 1 - slot)


def _moe_tables(routing, n_tok, tm):
    w_lo = routing[:, 0, :].reshape(n_tok)
    w_hi = routing[:, 1, :].reshape(n_tok)
    cls = routing[:, 2, :].reshape(n_tok).astype(I32)
    key = cls * n_tok + jnp.arange(n_tok, dtype=I32)
    skey, w_lo, w_hi = lax.sort((key, w_lo, w_hi), num_keys=1)
    tok_sorted = skey % n_tok
    bounds = jnp.arange(N_CLASSES + 1, dtype=I32) * n_tok
    cstart = jnp.sum((skey[None, :] < bounds[:, None]).astype(I32), axis=1)
    n_tiles = n_tok // tm
    cuts = jnp.sort(jnp.concatenate([jnp.arange(n_tiles, dtype=I32) * tm, cstart[1:N_CLASSES]]))
    ends = jnp.concatenate([cuts[1:], jnp.full((1,), n_tok, I32)])
    tile = jnp.minimum(cuts // tm, n_tiles - 1)
    c = jnp.clip(jnp.sum((cstart[None, :] <= cuts[:, None]).astype(I32), axis=1) - 1, 0, N_CLASSES - 1)
    lo = cuts - tile * tm
    hi = ends - tile * tm
    change = (tile[1:] != tile[:-1]).astype(I32)
    one = jnp.ones((1,), I32)
    first = jnp.concatenate([one, change])
    last = jnp.concatenate([change, one])
    grp = c // 6
    e_lo = EXPERTS_PER_GROUP * grp + jnp.asarray(PAIR_LO)[c % 6]
    e_hi = EXPERTS_PER_GROUP * grp + jnp.asarray(PAIR_HI)[c % 6]
    scalars = tuple(v.astype(I32) for v in (tok_sorted, tile, first, last, lo, hi, e_lo, e_hi))
    return scalars, jnp.stack([w_lo, w_hi], axis=1)


def _moe(h_tiles, routing, w_gate, w_up, w_down):
    n_tok = h_tiles.shape[0] // TOKEN_ROWS
    tm = TM_MOE
    scalars, w_pair = _moe_tables(routing, n_tok, tm)
    n_steps = scalars[1].shape[0]
    by_tile = lambda k, tok, tile, first, last, lo, hi, elo, ehi: (tile[k], 0)
    e_lo = lambda k, tok, tile, first, last, lo, hi, elo, ehi: (elo[k], 0, 0)
    e_hi = lambda k, tok, tile, first, last, lo, hi, elo, ehi: (ehi[k], 0, 0)
    up_blk = (1, D_MODEL, D_EXPERT)
    down_blk = (1, D_EXPERT, D_MODEL)
    grid_spec = pltpu.PrefetchScalarGridSpec(
        num_scalar_prefetch=len(scalars),
        grid=(n_steps,),
        in_specs=[
            pl.BlockSpec(memory_space=pl.ANY),
            pl.BlockSpec((tm, 2), by_tile),
            pl.BlockSpec(up_blk, e_lo), pl.BlockSpec(up_blk, e_lo), pl.BlockSpec(down_blk, e_lo),
            pl.BlockSpec(up_blk, e_hi), pl.BlockSpec(up_blk, e_hi), pl.BlockSpec(down_blk, e_hi),
        ],
        out_specs=pl.BlockSpec(memory_space=pl.ANY),
        scratch_shapes=[
            pltpu.VMEM((2, tm * TOKEN_ROWS, LANES), F32),
            pltpu.VMEM((2, tm * TOKEN_ROWS, LANES), F32),
            pltpu.VMEM((tm, D_MODEL), F32),
            pltpu.SemaphoreType.DMA((2,)),
            pltpu.SemaphoreType.DMA((2,)),
        ],
    )
    return pl.pallas_call(
        _moe_kernel,
        grid_spec=grid_spec,
        out_shape=jax.ShapeDtypeStruct((n_tok * TOKEN_ROWS, LANES), F32),
        compiler_params=_params(),
        name="moe",
    )(*scalars, h_tiles, w_pair, w_gate, w_up, w_down, w_gate, w_up, w_down)


def _final_kernel(x_ref, y_ref, g_ref, o_ref):
    x = x_ref[...] + _load_token_tiles(y_ref, x_ref.shape[0])
    o_ref[...] = (x * _rms_scale(x)) * g_ref[...]


def _final_norm(x, y, g):
    t = x.shape[0]
    tm = TM_DENSE
    row = lambda i: (i, 0)
    return pl.pallas_call(
        _final_kernel,
        grid=(t // tm,),
        in_specs=[pl.BlockSpec((tm, D_MODEL), row), pl.BlockSpec((tm * TOKEN_ROWS, LANES), row),
                  pl.BlockSpec((1, D_MODEL), lambda i: (0, 0))],
        out_specs=pl.BlockSpec((tm, D_MODEL), row),
        out_shape=jax.ShapeDtypeStruct((t, D_MODEL), F32),
        compiler_params=_params(),
        name="final_norm",
    )(x, y, g)


def _split_bf16(w):
    hi = w.astype(BF16)
    lo = (w - hi.astype(F32)).astype(BF16)
    return hi, lo


def kernel(x, norm_mix_g, w_in, rpb, pool_w, pool_scale, w_out, norm_ffn_g,
           w_router_group, w_router_expert, w_gate, w_up, w_down, final_g):
    batch, seq, d = x.shape
    depth = w_in.shape[0]
    assert d == D_MODEL and seq == GRID_H * GRID_W
    n_tok = batch * seq
    assert n_tok % TM_DENSE == 0 and n_tok % TM_MOE == 0

    bias = _attention_bias(rpb)
    cnt = jnp.asarray(_pool_counts(seq))
    w_in_b = w_in.astype(BF16)
    w_out_b = w_out.astype(BF16)
    pool_w_b = pool_w.astype(BF16)
    pool_scale3 = pool_scale.reshape(depth, D_POOL // POOL_GROUP_DIM, 1, POOL_GROUP_DIM)
    w_gate_b = w_gate.astype(BF16)
    w_up_b = w_up.astype(BF16)
    w_down_b = w_down.astype(BF16)
    w_router = jnp.concatenate([w_router_group, w_router_expert], axis=-1).transpose(0, 2, 1)
    w_router = jnp.pad(w_router, ((0, 0), (0, ROUTER_ROWS - w_router.shape[1]), (0, 0)))
    w_router_b = jnp.concatenate(_split_bf16(w_router), axis=1)

    xf = x.reshape(n_tok, D_MODEL)
    y = None
    for l in range(depth):
        xf, proj = _in_proj(xf, y, norm_mix_g[l].reshape(1, D_MODEL), w_in_b[l])
        a, p = _mixer(proj, bias[l], cnt, pool_w_b[l], pool_scale3[l], batch, seq)
        xf, h_tiles, routing = _out_proj(a, p, xf, w_out_b[l, :D_ATTN], w_out_b[l, D_ATTN:],
                                         norm_ffn_g[l].reshape(1, D_MODEL), w_router_b[l])
        y = _moe(h_tiles, routing, w_gate_b[l], w_up_b[l], w_down_b[l])
    out = _final_norm(xf, y, final_g.reshape(1, D_MODEL))
    return out.reshape(batch, seq, D_MODEL)
```

```python
import functools

import numpy as np
import jax
import jax.numpy as jnp
from jax import lax
from jax.experimental import pallas as pl
from jax.experimental.pallas import tpu as pltpu

F32 = jnp.float32
BF16 = jnp.bfloat16
I32 = jnp.int32

D_MODEL = 1024
GRID_W = 64
GRID_H = 32
D_ATTN = 512
HEAD_DIM = 64
N_HEADS = 8
WIN_H = 8
WIN_W = 16
D_POOL = 512
POOL_GROUP_DIM = 128
D_IN = 2048
N_GROUPS = 4
EXPERTS_PER_GROUP = 4
D_EXPERT = 512
N_CLASSES = 24
EPS = 1e-6
NEG_INF = -1e30

LANES = 128
SUBLANES = 8
TOKEN_ROWS = D_MODEL // LANES
COL_BLOCK = 256
HEADS_PER_BLOCK = COL_BLOCK // HEAD_DIM
GROUPS_PER_BLOCK = COL_BLOCK // POOL_GROUP_DIM
N_COL_BLOCKS = D_ATTN // COL_BLOCK
Q_ROWS = 8
Q_COLS = 16
K_ROWS = 16
K_COLS = 32
Q_ROT = 8
ROW_BLOCK_KEY_START = (0, 4, 12, 16)
POOL_HALO = 16
POOL_WINDOWS = (2, 4, 8, 16)

TM_DENSE = 512
TM_MOE = 256
VMEM_LIMIT = 56 * 1024 * 1024

PAIR_LO = np.array([0, 0, 0, 1, 1, 2], np.int32)
PAIR_HI = np.array([1, 2, 3, 2, 3, 3], np.int32)


def _params(n_axes=1):
    return pltpu.CompilerParams(dimension_semantics=("arbitrary",) * n_axes, vmem_limit_bytes=VMEM_LIMIT)


def _rms_scale(x):
    return lax.rsqrt(jnp.mean(x * x, axis=-1, keepdims=True) + EPS)


def _load_token_tiles(ref, n_tok):
    return jnp.concatenate([ref[pl.ds(s, n_tok, stride=TOKEN_ROWS), :] for s in range(TOKEN_ROWS)], axis=1)


def _store_token_tiles(ref, val):
    n_tok = val.shape[0]
    for s in range(TOKEN_ROWS):
        ref[pl.ds(s, n_tok, stride=TOKEN_ROWS), :] = val[:, s * LANES:(s + 1) * LANES]


def _in_kernel(has_y, *refs):
    if has_y:
        x_ref, y_ref, g_ref, w_ref, xo_ref, proj_ref, w_bf = refs
    else:
        x_ref, g_ref, w_ref, proj_ref, w_bf = refs

    @pl.when(pl.program_id(0) == 0)
    def _():
        w_bf[...] = w_ref[0].astype(BF16)

    if has_y:
        x = x_ref[...] + _load_token_tiles(y_ref, x_ref.shape[0])
        xo_ref[...] = x
    else:
        x = x_ref[...]
    tm = x.shape[0]
    h = ((x * _rms_scale(x)) * g_ref[0]).astype(BF16)
    proj = jnp.dot(h, w_bf[...], preferred_element_type=F32)
    q = (proj[:, :D_ATTN] * (HEAD_DIM ** -0.5)).reshape(tm // GRID_W, GRID_W, D_ATTN)
    q = jnp.concatenate([q[:, Q_ROT:], q[:, :Q_ROT]], axis=1).reshape(tm, D_ATTN)
    proj_ref[:, :D_ATTN] = q.astype(BF16)
    proj_ref[:, D_ATTN:] = proj[:, D_ATTN:].astype(BF16)


def _in_proj(x, y, g, w, layer):
    t = x.shape[0]
    tm = TM_DENSE
    row = lambda i: (i, 0)
    fixed = lambda i: (layer, 0, 0)
    in_specs = [pl.BlockSpec((tm, D_MODEL), row)]
    args = [x]
    out_shape = []
    out_specs = []
    if y is not None:
        in_specs.append(pl.BlockSpec((tm * TOKEN_ROWS, LANES), row))
        args.append(y)
        out_shape.append(jax.ShapeDtypeStruct((t, D_MODEL), F32))
        out_specs.append(pl.BlockSpec((tm, D_MODEL), row))
    in_specs += [pl.BlockSpec((1, 1, D_MODEL), fixed), pl.BlockSpec((1, D_MODEL, D_IN), fixed)]
    args += [g, w]
    out_shape.append(jax.ShapeDtypeStruct((t, D_IN), BF16))
    out_specs.append(pl.BlockSpec((tm, D_IN), row))
    res = pl.pallas_call(
        functools.partial(_in_kernel, y is not None),
        grid=(t // tm,),
        in_specs=in_specs,
        out_specs=out_specs,
        out_shape=out_shape,
        scratch_shapes=[pltpu.VMEM((D_MODEL, D_IN), BF16)],
        compiler_params=_params(),
        name="in_proj",
    )(*args)
    if y is None:
        return x, res[0]
    return res[0], res[1]


def _key_col_segments(m):
    if m < 3:
        return ((Q_COLS * m, K_COLS),)
    return ((0, K_COLS // 2), (GRID_W - K_COLS // 2, K_COLS // 2))


def _key_col(m, kk):
    for start, n in _key_col_segments(m):
        if kk < n:
            return start + kk
        kk -= n
    raise ValueError(kk)


def _attention_index_tables():
    dr = np.zeros((3, Q_ROWS, K_ROWS), np.int32)
    rv = np.zeros((3, Q_ROWS, K_ROWS), bool)
    for typ, rb in enumerate((0, 1, 3)):
        for i in range(Q_ROWS):
            r = Q_ROWS * rb + i
            r_start = min(max(r - WIN_H // 2, 0), GRID_H - WIN_H)
            for kr in range(K_ROWS):
                ka = ROW_BLOCK_KEY_START[rb] + kr
                rv[typ, i, kr] = r_start <= ka < r_start + WIN_H
                dr[typ, i, kr] = min(max(ka - r + WIN_H - 1, 0), 2 * WIN_H - 2)
    dc = np.zeros((2, Q_COLS, K_COLS), np.int32)
    cv = np.zeros((2, Q_COLS, K_COLS), bool)
    for mt, m in enumerate((0, 3)):
        for j in range(Q_COLS):
            qc = (Q_COLS * m + j + Q_ROT) % GRID_W
            c_start = min(max(qc - WIN_W // 2, 0), GRID_W - WIN_W)
            for kk in range(K_COLS):
                kc = _key_col(m, kk)
                cv[mt, j, kk] = c_start <= kc < c_start + WIN_W
                dc[mt, j, kk] = min(max(kc - qc, -(WIN_W - 1)), WIN_W - 1) + WIN_W - 1
    return dr, rv, dc, cv


def _attention_bias(rpb):
    dr, rv, dc, cv = _attention_index_tables()
    nl, nh, n_dr, n_dc = rpb.shape
    idx = dr[:, None, :, None, :, None] * n_dc + dc[None, :, None, :, None, :]
    valid = rv[:, None, :, None, :, None] & cv[None, :, None, :, None, :]
    idx = np.where(valid, idx, n_dr * n_dc).reshape(6, Q_ROWS * Q_COLS, K_ROWS * K_COLS).astype(np.int32)
    table = jnp.concatenate([rpb.reshape(nl * nh, n_dr * n_dc).astype(F32),
                             jnp.full((nl * nh, 1), NEG_INF, F32)], axis=1)
    return jnp.take(table, jnp.asarray(idx), axis=1)


def _pool_counts(seq):
    t = np.arange(seq)
    cols = []
    for w in POOL_WINDOWS:
        cnt = np.minimum(t + w // 2, seq) - np.maximum(t - w // 2, 0)
        cols.append(np.broadcast_to(cnt[:, None].astype(np.float32), (seq, POOL_GROUP_DIM)))
    return np.concatenate(cols, axis=1).reshape(seq, N_COL_BLOCKS, COL_BLOCK).transpose(1, 0, 2)


def _window_sum(ue, window):
    n = ue.shape[0]

    def shifted(x, d):
        return pltpu.roll(x, d % n, axis=0)

    s = ue + shifted(ue, 1)
    reach = 1
    while 2 * reach < window:
        s = shifted(s, reach) + shifted(s, -reach)
        reach *= 2
    return s


def _mix_kernel(q_ref, k_ref, v_ref, u_ref, bias_ref, cnt_ref, pw_ref, ps_ref, a_ref, p_ref):
    jb = pl.program_id(0)
    seq = q_ref.shape[1]
    nq = Q_ROWS * Q_COLS
    nk = K_ROWS * K_COLS
    lane_head = lax.broadcasted_iota(I32, (nq, COL_BLOCK), 1) // HEAD_DIM

    def rows(ref, grid_row, col, n):
        return ref[0, pl.ds(pl.multiple_of(grid_row * GRID_W + col, Q_COLS), n), :]

    def row_block(rb, carry):
        key_row0 = jnp.clip(Q_ROWS * rb - WIN_H // 2, 0, GRID_H - K_ROWS)
        typ = jnp.where(rb == 0, 0, jnp.where(rb == GRID_H // Q_ROWS - 1, 2, 1))
        outs = []
        for m in range(GRID_W // Q_COLS):
            mt = 0 if m < 3 else 1
            qb = jnp.concatenate([rows(q_ref, Q_ROWS * rb + i, Q_COLS * m, Q_COLS) for i in range(Q_ROWS)], axis=0)
            segs = _key_col_segments(m)
            kb = jnp.concatenate(
                [rows(k_ref, key_row0 + kr, c0, n) for kr in range(K_ROWS) for c0, n in segs], axis=0)
            vb = jnp.concatenate(
                [rows(v_ref, key_row0 + kr, c0, n) for kr in range(K_ROWS) for c0, n in segs], axis=0)
            zero = jnp.zeros_like(qb)
            qh = jnp.concatenate([jnp.where(lane_head == h, qb, zero) for h in range(HEADS_PER_BLOCK)], axis=0)
            s = lax.dot_general(qh, kb, (((1,), (1,)), ((), ())), preferred_element_type=F32)
            s = s + jnp.concatenate([bias_ref[h, 2 * typ + mt] for h in range(HEADS_PER_BLOCK)], axis=0)
            mx = jnp.max(s, axis=-1, keepdims=True)
            e = jnp.exp(s - mx)
            den = jnp.sum(e, axis=-1, keepdims=True)
            oh = jnp.dot(e.astype(BF16), vb, preferred_element_type=F32)
            o = oh[:nq]
            dn = jnp.broadcast_to(den[:nq], (nq, COL_BLOCK))
            for h in range(1, HEADS_PER_BLOCK):
                o = jnp.where(lane_head == h, oh[h * nq:(h + 1) * nq], o)
                dn = jnp.where(lane_head == h, den[h * nq:(h + 1) * nq], dn)
            outs.append(o / dn)
        for i in range(Q_ROWS):
            pieces = []
            for m in range(GRID_W // Q_COLS):
                pieces.append(outs[(m - 1) % 4][Q_COLS * i + Q_ROT:Q_COLS * (i + 1)])
                pieces.append(outs[m][Q_COLS * i:Q_COLS * i + Q_ROT])
            start = pl.multiple_of((Q_ROWS * rb + i) * GRID_W, GRID_W)
            a_ref[0, pl.ds(start, GRID_W), :] = jnp.concatenate(pieces, axis=0).astype(BF16)
        return carry

    lax.fori_loop(0, GRID_H // Q_ROWS, row_block, 0)

    def pool(windows):
        u = u_ref[0].astype(F32)
        pad = jnp.zeros((POOL_HALO, POOL_GROUP_DIM), F32)
        for gi, window in enumerate(windows):
            sl = slice(gi * POOL_GROUP_DIM, (gi + 1) * POOL_GROUP_DIM)
            ug = u[:, sl]
            win = _window_sum(jnp.concatenate([pad, ug, pad], axis=0), window)[POOL_HALO:POOL_HALO + seq]
            pooled = win / cnt_ref[0, :, sl] - ug
            mixed = jnp.dot(pooled.astype(BF16), pw_ref[0, gi].astype(BF16), preferred_element_type=F32) * ps_ref[0, gi]
            p_ref[0, :, sl] = mixed.astype(BF16)

    for cb in range(N_COL_BLOCKS):
        @pl.when(jb == cb)
        def _(cb=cb):
            pool(POOL_WINDOWS[cb * GROUPS_PER_BLOCK:(cb + 1) * GROUPS_PER_BLOCK])


def _mixer(proj, bias, cnt, pool_w, pool_scale, layer, batch, seq):
    proj3 = proj.reshape(batch, seq, D_IN)
    col = lambda base: (lambda j, b: (b, 0, base + j))
    blk = (1, seq, COL_BLOCK)
    bias_blk = (HEADS_PER_BLOCK,) + bias.shape[1:]
    a, p = pl.pallas_call(
        _mix_kernel,
        grid=(N_COL_BLOCKS, batch),
        in_specs=[
            pl.BlockSpec(blk, col(0)),
            pl.BlockSpec(blk, col(N_COL_BLOCKS)),
            pl.BlockSpec(blk, col(2 * N_COL_BLOCKS)),
            pl.BlockSpec(blk, col(3 * N_COL_BLOCKS)),
            pl.BlockSpec(bias_blk, lambda j, b: (layer * N_COL_BLOCKS + j, 0, 0, 0)),
            pl.BlockSpec(blk, lambda j, b: (j, 0, 0)),
            pl.BlockSpec((1, GROUPS_PER_BLOCK, POOL_GROUP_DIM, POOL_GROUP_DIM), lambda j, b: (layer, j, 0, 0)),
            pl.BlockSpec((1, GROUPS_PER_BLOCK, 1, POOL_GROUP_DIM), lambda j, b: (layer, j, 0, 0)),
        ],
        out_specs=[pl.BlockSpec(blk, lambda j, b: (b, 0, j)), pl.BlockSpec(blk, lambda j, b: (b, 0, j))],
        out_shape=[jax.ShapeDtypeStruct((batch, seq, D_ATTN), BF16), jax.ShapeDtypeStruct((batch, seq, D_POOL), BF16)],
        compiler_params=_params(2),
        name="mixer",
    )(proj3, proj3, proj3, proj3, bias, cnt, pool_w, pool_scale)
    return a.reshape(batch * seq, D_ATTN), p.reshape(batch * seq, D_POOL)


ROUTER_ROWS = 32
OUT_SPLIT = 2


def _first_max(vals):
    best = vals[0]
    for v in vals[1:]:
        best = jnp.maximum(best, v)
    idx = jnp.float32(len(vals) - 1)
    for i in range(len(vals) - 2, -1, -1):
        idx = jnp.where(vals[i] == best, jnp.float32(i), idx)
    return best, idx


def _route(lg):
    n = lg.shape[1]
    big = jnp.float32(-3e38)
    g = [lg[i:i + 1] for i in range(N_GROUPS)]
    gmax, gidx = _first_max(g)
    gsum = jnp.exp(g[0] - gmax)
    for v in g[1:]:
        gsum = gsum + jnp.exp(v - gmax)
    gw = 1.0 / gsum
    e = []
    for j in range(EXPERTS_PER_GROUP):
        v = lg[N_GROUPS + j:N_GROUPS + j + 1]
        for gg in range(1, N_GROUPS):
            row = N_GROUPS + EXPERTS_PER_GROUP * gg + j
            v = jnp.where(gidx == gg, lg[row:row + 1], v)
        e.append(v)
    v1, i1 = _first_max(e)
    v2, i2 = _first_max([jnp.where(i1 == j, big, e[j]) for j in range(EXPERTS_PER_GROUP)])
    d = jnp.exp(v2 - v1)
    w1 = gw * (1.0 / (1.0 + d))
    w2 = gw * (d / (1.0 + d))
    first_lo = i1 < i2
    lo = jnp.where(first_lo, i1, i2)
    hi = jnp.where(first_lo, i2, i1)
    w_lo = jnp.where(first_lo, w1, w2)
    w_hi = jnp.where(first_lo, w2, w1)
    cls = 6.0 * gidx + 0.5 * (lo * (7.0 - lo)) + (hi - lo - 1.0)
    return jnp.concatenate([w_lo, w_hi, cls, jnp.zeros((SUBLANES - 3, n), F32)], axis=0)


def _out_kernel(a_ref, p_ref, x_ref, wa_ref, wp_ref, g_ref, wr_ref, x2_ref, h_ref, r_ref, wa_bf, wp_bf):
    @pl.when(pl.program_id(0) == 0)
    def _():
        wa_bf[...] = wa_ref[0].astype(BF16)
        wp_bf[...] = wp_ref[0].astype(BF16)

    tm = x_ref.shape[0]
    n = tm // OUT_SPLIT
    nt_dims = (((1,), (1,)), ((), ()))
    for c in range(OUT_SPLIT):
        sl = pl.ds(c * n, n)
        mix = jnp.dot(a_ref[sl, :], wa_bf[...], preferred_element_type=F32)
        mix = mix + jnp.dot(p_ref[sl, :], wp_bf[...], preferred_element_type=F32)
        x2 = x_ref[sl, :] + mix
        x2_ref[sl, :] = x2
        h = (x2 * _rms_scale(x2)) * g_ref[0]
        _store_token_tiles(h_ref.at[pl.ds(c * n * TOKEN_ROWS, n * TOKEN_ROWS)], h)
        hh = h.astype(BF16)
        hl = (h - hh.astype(F32)).astype(BF16)
        r1 = lax.dot_general(wr_ref[0], hh, nt_dims, preferred_element_type=F32)
        r2 = lax.dot_general(wr_ref[0, :ROUTER_ROWS], hl, nt_dims, preferred_element_type=F32)
        r_ref[0, :, sl] = _route(r1[:ROUTER_ROWS] + r1[ROUTER_ROWS:] + r2)


def _out_proj(a, p, x, w_out, g, wr, layer):
    t = x.shape[0]
    tm = TM_DENSE
    row = lambda i: (i, 0)
    fixed = lambda i: (layer, 0, 0)
    return pl.pallas_call(
        _out_kernel,
        grid=(t // tm,),
        in_specs=[
            pl.BlockSpec((tm, D_ATTN), row),
            pl.BlockSpec((tm, D_POOL), row),
            pl.BlockSpec((tm, D_MODEL), row),
            pl.BlockSpec((1, D_ATTN, D_MODEL), fixed),
            pl.BlockSpec((1, D_POOL, D_MODEL), lambda i: (layer, 1, 0)),
            pl.BlockSpec((1, 1, D_MODEL), fixed),
            pl.BlockSpec((1, 2 * ROUTER_ROWS, D_MODEL), fixed),
        ],
        out_specs=[pl.BlockSpec((tm, D_MODEL), row),
                   pl.BlockSpec((tm * TOKEN_ROWS, LANES), row),
                   pl.BlockSpec((1, SUBLANES, tm), lambda i: (i, 0, 0))],
        out_shape=[jax.ShapeDtypeStruct((t, D_MODEL), F32),
                   jax.ShapeDtypeStruct((t * TOKEN_ROWS, LANES), F32),
                   jax.ShapeDtypeStruct((t // tm, SUBLANES, tm), F32)],
        scratch_shapes=[pltpu.VMEM((D_ATTN, D_MODEL), BF16), pltpu.VMEM((D_POOL, D_MODEL), BF16)],
        compiler_params=_params(),
        name="out_proj",
    )(a, p, x, w_out, w_out, g, wr)


def _moe_kernel(tok_ref, tile_ref, first_ref, last_ref, lo_ref, hi_ref, newc_ref, elo_ref, ehi_ref,
                h_hbm, w_ref, wg_lo, wu_lo, wd_lo, wg_hi, wu_hi, wd_hi, y_hbm,
                xbuf, ybuf, acc, wg_bf, wu_bf, wd_bf, gsem, ssem):
    del elo_ref, ehi_ref
    k = pl.program_id(0)
    tm = acc.shape[0]
    n_tiles = y_hbm.shape[0] // (tm * TOKEN_ROWS)
    tile = tile_ref[k]
    slot = tile % 2

    @pl.when(newc_ref[k] == 1)
    def _():
        for e, (wg, wu, wd) in enumerate(((wg_lo, wu_lo, wd_lo), (wg_hi, wu_hi, wd_hi))):
            wg_bf[e] = wg[0].astype(BF16)
            wu_bf[e] = wu[0].astype(BF16)
            wd_bf[e] = wd[0].astype(BF16)

    def token_rows(idx):
        return pl.ds(pl.multiple_of(idx * TOKEN_ROWS, TOKEN_ROWS), TOKEN_ROWS)

    def gather_copy(j, tok, sl):
        return pltpu.make_async_copy(h_hbm.at[token_rows(tok)], xbuf.at[sl, token_rows(j)], gsem.at[sl])

    def scatter_copy(j, tok, sl):
        return pltpu.make_async_copy(ybuf.at[sl, token_rows(j)], y_hbm.at[token_rows(tok)], ssem.at[sl])

    def start_rows(copy, tl, sl):
        def body(j, c):
            copy(j, tok_ref[tl * tm + j], sl).start()
            return c
        lax.fori_loop(0, tm, body, 0, unroll=8)

    def wait_rows(copy, sl):
        for j in range(tm):
            copy(j, 0, sl).wait()

    @pl.when(first_ref[k] == 1)
    def _():
        @pl.when(k == 0)
        def _():
            start_rows(gather_copy, tile, slot)
        wait_rows(gather_copy, slot)

        @pl.when(tile + 1 < n_tiles)
        def _():
            start_rows(gather_copy, tile + 1, 1 - slot)

        @pl.when(tile >= 2)
        def _():
            wait_rows(scatter_copy, slot)
        acc[...] = jnp.zeros_like(acc)

    lo = lo_ref[k]
    hi = hi_ref[k]

    @pl.when(hi > lo)
    def _():
        x = _load_token_tiles(xbuf.at[slot], tm).astype(BF16)
        row = lax.broadcasted_iota(I32, (tm, 1), 0)
        mine = (row >= lo) & (row < hi)
        w = w_ref[...]
        w_lo = jnp.where(mine, w[:, 0:1], 0.0)
        w_hi = jnp.where(mine, w[:, 1:2], 0.0)

        def expert(e):
            a = jax.nn.silu(jnp.dot(x, wg_bf[e], preferred_element_type=F32)) * jnp.dot(x, wu_bf[e], preferred_element_type=F32)
            return jnp.dot(a.astype(BF16), wd_bf[e], preferred_element_type=F32)

        acc[...] += w_lo * expert(0) + w_hi * expert(1)

    @pl.when(last_ref[k] == 1)
    def _():
        _store_token_tiles(ybuf.at[slot], acc[...])
        start_rows(scatter_copy, tile, slot)

    @pl.when(k == pl.num_programs(0) - 1)
    def _():
        wait_rows(scatter_copy, slot)
        if n_tiles > 1:
            wait_rows(scatter_copy, 1 - slot)


def _moe_tables(routing, n_tok, tm, expert0):
    w_lo = routing[:, 0, :].reshape(n_tok)
    w_hi = routing[:, 1, :].reshape(n_tok)
    cls = routing[:, 2, :].reshape(n_tok).astype(I32)
    key = cls * n_tok + jnp.arange(n_tok, dtype=I32)
    skey, w_lo, w_hi = lax.sort((key, w_lo, w_hi), num_keys=1)
    tok_sorted = skey % n_tok
    bounds = jnp.arange(N_CLASSES + 1, dtype=I32) * n_tok
    cstart = jnp.sum((skey[None, :] < bounds[:, None]).astype(I32), axis=1)
    n_tiles = n_tok // tm
    cuts = jnp.sort(jnp.concatenate([jnp.arange(n_tiles, dtype=I32) * tm, cstart[1:N_CLASSES]]))
    ends = jnp.concatenate([cuts[1:], jnp.full((1,), n_tok, I32)])
    tile = jnp.minimum(cuts // tm, n_tiles - 1)
    c = jnp.clip(jnp.sum((cstart[None, :] <= cuts[:, None]).astype(I32), axis=1) - 1, 0, N_CLASSES - 1)
    lo = cuts - tile * tm
    hi = ends - tile * tm
    change = (tile[1:] != tile[:-1]).astype(I32)
    one = jnp.ones((1,), I32)
    first = jnp.concatenate([one, change])
    last = jnp.concatenate([change, one])
    new_class = jnp.concatenate([one, (c[1:] != c[:-1]).astype(I32)])
    grp = c // 6
    e_lo = expert0 + EXPERTS_PER_GROUP * grp + jnp.asarray(PAIR_LO)[c % 6]
    e_hi = expert0 + EXPERTS_PER_GROUP * grp + jnp.asarray(PAIR_HI)[c % 6]
    scalars = tuple(v.astype(I32) for v in (tok_sorted, tile, first, last, lo, hi, new_class, e_lo, e_hi))
    return scalars, jnp.stack([w_lo, w_hi], axis=1)


def _moe(h_tiles, routing, w_gate, w_up, w_down, layer):
    n_tok = h_tiles.shape[0] // TOKEN_ROWS
    tm = TM_MOE
    scalars, w_pair = _moe_tables(routing, n_tok, tm, layer * N_GROUPS * EXPERTS_PER_GROUP)
    n_steps = scalars[1].shape[0]
    by_tile = lambda k, tok, tile, first, last, lo, hi, newc, elo, ehi: (tile[k], 0)
    e_lo = lambda k, tok, tile, first, last, lo, hi, newc, elo, ehi: (elo[k], 0, 0)
    e_hi = lambda k, tok, tile, first, last, lo, hi, newc, elo, ehi: (ehi[k], 0, 0)
    up_blk = (1, D_MODEL, D_EXPERT)
    down_blk = (1, D_EXPERT, D_MODEL)
    grid_spec = pltpu.PrefetchScalarGridSpec(
        num_scalar_prefetch=len(scalars),
        grid=(n_steps,),
        in_specs=[
            pl.BlockSpec(memory_space=pl.ANY),
            pl.BlockSpec((tm, 2), by_tile),
            pl.BlockSpec(up_blk, e_lo), pl.BlockSpec(up_blk, e_lo), pl.BlockSpec(down_blk, e_lo),
            pl.BlockSpec(up_blk, e_hi), pl.BlockSpec(up_blk, e_hi), pl.BlockSpec(down_blk, e_hi),
        ],
        out_specs=pl.BlockSpec(memory_space=pl.ANY),
        scratch_shapes=[
            pltpu.VMEM((2, tm * TOKEN_ROWS, LANES), F32),
            pltpu.VMEM((2, tm * TOKEN_ROWS, LANES), F32),
            pltpu.VMEM((tm, D_MODEL), F32),
            pltpu.VMEM((2, D_MODEL, D_EXPERT), BF16),
            pltpu.VMEM((2, D_MODEL, D_EXPERT), BF16),
            pltpu.VMEM((2, D_EXPERT, D_MODEL), BF16),
            pltpu.SemaphoreType.DMA((2,)),
            pltpu.SemaphoreType.DMA((2,)),
        ],
    )
    return pl.pallas_call(
        _moe_kernel,
        grid_spec=grid_spec,
        out_shape=jax.ShapeDtypeStruct((n_tok * TOKEN_ROWS, LANES), F32),
        compiler_params=_params(),
        name="moe",
    )(*scalars, h_tiles, w_pair, w_gate, w_up, w_down, w_gate, w_up, w_down)


def _final_kernel(x_ref, y_ref, g_ref, o_ref):
    x = x_ref[...] + _load_token_tiles(y_ref, x_ref.shape[0])
    o_ref[...] = (x * _rms_scale(x)) * g_ref[...]


def _final_norm(x, y, g):
    t = x.shape[0]
    tm = TM_DENSE
    row = lambda i: (i, 0)
    return pl.pallas_call(
        _final_kernel,
        grid=(t // tm,),
        in_specs=[pl.BlockSpec((tm, D_MODEL), row), pl.BlockSpec((tm * TOKEN_ROWS, LANES), row),
                  pl.BlockSpec((1, D_MODEL), lambda i: (0, 0))],
        out_specs=pl.BlockSpec((tm, D_MODEL), row),
        out_shape=jax.ShapeDtypeStruct((t, D_MODEL), F32),
        compiler_params=_params(),
        name="final_norm",
    )(x, y, g)


def _split_bf16(w):
    hi = w.astype(BF16)
    lo = (w - hi.astype(F32)).astype(BF16)
    return hi, lo


def kernel(x, norm_mix_g, w_in, rpb, pool_w, pool_scale, w_out, norm_ffn_g,
           w_router_group, w_router_expert, w_gate, w_up, w_down, final_g):
    batch, seq, d = x.shape
    depth = w_in.shape[0]
    assert d == D_MODEL and seq == GRID_H * GRID_W
    n_tok = batch * seq
    assert n_tok % TM_DENSE == 0 and n_tok % TM_MOE == 0

    bias = _attention_bias(rpb)
    cnt = jnp.asarray(_pool_counts(seq))
    n_pool = D_POOL // POOL_GROUP_DIM
    pool_scale4 = pool_scale.reshape(depth, n_pool, 1, POOL_GROUP_DIM)
    g_mix = norm_mix_g.reshape(depth, 1, D_MODEL)
    g_ffn = norm_ffn_g.reshape(depth, 1, D_MODEL)
    n_exp = N_GROUPS * EXPERTS_PER_GROUP
    w_gate_s = w_gate.reshape(depth * n_exp, D_MODEL, D_EXPERT)
    w_up_s = w_up.reshape(depth * n_exp, D_MODEL, D_EXPERT)
    w_down_s = w_down.reshape(depth * n_exp, D_EXPERT, D_MODEL)
    w_router = jnp.concatenate([w_router_group, w_router_expert], axis=-1).transpose(0, 2, 1)
    w_router = jnp.pad(w_router, ((0, 0), (0, ROUTER_ROWS - w_router.shape[1]), (0, 0)))
    w_router_b = jnp.concatenate(_split_bf16(w_router), axis=1)

    xf = x.reshape(n_tok, D_MODEL)
    y = None
    for l in range(depth):
        xf, proj = _in_proj(xf, y, g_mix, w_in, l)
        a, p = _mixer(proj, bias, cnt, pool_w, pool_scale4, l, batch, seq)
        xf, h_tiles, routing = _out_proj(a, p, xf, w_out, g_ffn, w_router_b, l)
        y = _moe(h_tiles, routing, w_gate_s, w_up_s, w_down_s, l)
    out = _final_norm(xf, y, final_g.reshape(1, D_MODEL))
    return out.reshape(batch, seq, D_MODEL)
```

```python
import functools

import numpy as np
import jax
import jax.numpy as jnp
from jax import lax
from jax.experimental import pallas as pl
from jax.experimental.pallas import tpu as pltpu

F32 = jnp.float32
BF16 = jnp.bfloat16
I32 = jnp.int32

D_MODEL = 1024
GRID_W = 64
GRID_H = 32
D_ATTN = 512
HEAD_DIM = 64
N_HEADS = 8
WIN_H = 8
WIN_W = 16
D_POOL = 512
POOL_GROUP_DIM = 128
D_IN = 2048
N_GROUPS = 4
EXPERTS_PER_GROUP = 4
D_EXPERT = 512
N_CLASSES = 24
EPS = 1e-6
NEG_INF = -1e30

LANES = 128
SUBLANES = 8
TOKEN_ROWS = D_MODEL // LANES
COL_BLOCK = 256
HEADS_PER_BLOCK = COL_BLOCK // HEAD_DIM
GROUPS_PER_BLOCK = COL_BLOCK // POOL_GROUP_DIM
N_COL_BLOCKS = D_ATTN // COL_BLOCK
Q_ROWS = 8
Q_COLS = 16
K_ROWS = 16
K_COLS = 32
Q_ROT = 8
ROW_BLOCK_KEY_START = (0, 4, 12, 16)
POOL_HALO = 16
POOL_WINDOWS = (2, 4, 8, 16)

TM_DENSE = 512
TM_MOE = 256
VMEM_LIMIT = 56 * 1024 * 1024

PAIR_LO = np.array([0, 0, 0, 1, 1, 2], np.int32)
PAIR_HI = np.array([1, 2, 3, 2, 3, 3], np.int32)


def _params(n_axes=1):
    return pltpu.CompilerParams(dimension_semantics=("arbitrary",) * n_axes, vmem_limit_bytes=VMEM_LIMIT)


def _rms_scale(x):
    return lax.rsqrt(jnp.mean(x * x, axis=-1, keepdims=True) + EPS)


def _load_token_tiles(ref, n_tok):
    return jnp.concatenate([ref[pl.ds(s, n_tok, stride=TOKEN_ROWS), :] for s in range(TOKEN_ROWS)], axis=1)


def _store_token_tiles(ref, val):
    n_tok = val.shape[0]
    for s in range(TOKEN_ROWS):
        ref[pl.ds(s, n_tok, stride=TOKEN_ROWS), :] = val[:, s * LANES:(s + 1) * LANES]


def _in_kernel(has_y, *refs):
    if has_y:
        x_ref, y_ref, g_ref, w_ref, xo_ref, proj_ref, w_bf = refs
    else:
        x_ref, g_ref, w_ref, proj_ref, w_bf = refs

    @pl.when(pl.program_id(0) == 0)
    def _():
        w_bf[...] = w_ref[0].astype(BF16)

    if has_y:
        x = x_ref[...] + _load_token_tiles(y_ref, x_ref.shape[0])
        xo_ref[...] = x
    else:
        x = x_ref[...]
    tm = x.shape[0]
    h = ((x * _rms_scale(x)) * g_ref[0]).astype(BF16)
    proj = jnp.dot(h, w_bf[...], preferred_element_type=F32)
    q = (proj[:, :D_ATTN] * (HEAD_DIM ** -0.5)).reshape(tm // GRID_W, GRID_W, D_ATTN)
    q = jnp.concatenate([q[:, Q_ROT:], q[:, :Q_ROT]], axis=1).reshape(tm, D_ATTN)
    proj_ref[:, :D_ATTN] = q.astype(BF16)
    proj_ref[:, D_ATTN:] = proj[:, D_ATTN:].astype(BF16)


def _in_proj(x, y, g, w, layer):
    t = x.shape[0]
    tm = TM_DENSE
    row = lambda i: (i, 0)
    fixed = lambda i: (layer, 0, 0)
    in_specs = [pl.BlockSpec((tm, D_MODEL), row)]
    args = [x]
    out_shape = []
    out_specs = []
    if y is not None:
        in_specs.append(pl.BlockSpec((tm * TOKEN_ROWS, LANES), row))
        args.append(y)
        out_shape.append(jax.ShapeDtypeStruct((t, D_MODEL), F32))
        out_specs.append(pl.BlockSpec((tm, D_MODEL), row))
    in_specs += [pl.BlockSpec((1, 1, D_MODEL), fixed), pl.BlockSpec((1, D_MODEL, D_IN), fixed)]
    args += [g, w]
    out_shape.append(jax.ShapeDtypeStruct((t, D_IN), BF16))
    out_specs.append(pl.BlockSpec((tm, D_IN), row))
    res = pl.pallas_call(
        functools.partial(_in_kernel, y is not None),
        grid=(t // tm,),
        in_specs=in_specs,
        out_specs=out_specs,
        out_shape=out_shape,
        scratch_shapes=[pltpu.VMEM((D_MODEL, D_IN), BF16)],
        compiler_params=_params(),
        name="in_proj",
    )(*args)
    if y is None:
        return x, res[0]
    return res[0], res[1]


def _key_col_segments(m):
    if m < 3:
        return ((Q_COLS * m, K_COLS),)
    return ((0, K_COLS // 2), (GRID_W - K_COLS // 2, K_COLS // 2))


def _key_col(m, kk):
    for start, n in _key_col_segments(m):
        if kk < n:
            return start + kk
        kk -= n
    raise ValueError(kk)


def _attention_index_tables():
    dr = np.zeros((3, Q_ROWS, K_ROWS), np.int32)
    rv = np.zeros((3, Q_ROWS, K_ROWS), bool)
    for typ, rb in enumerate((0, 1, 3)):
        for i in range(Q_ROWS):
            r = Q_ROWS * rb + i
            r_start = min(max(r - WIN_H // 2, 0), GRID_H - WIN_H)
            for kr in range(K_ROWS):
                ka = ROW_BLOCK_KEY_START[rb] + kr
                rv[typ, i, kr] = r_start <= ka < r_start + WIN_H
                dr[typ, i, kr] = min(max(ka - r + WIN_H - 1, 0), 2 * WIN_H - 2)
    dc = np.zeros((2, Q_COLS, K_COLS), np.int32)
    cv = np.zeros((2, Q_COLS, K_COLS), bool)
    for mt, m in enumerate((0, 3)):
        for j in range(Q_COLS):
            qc = (Q_COLS * m + j + Q_ROT) % GRID_W
            c_start = min(max(qc - WIN_W // 2, 0), GRID_W - WIN_W)
            for kk in range(K_COLS):
                kc = _key_col(m, kk)
                cv[mt, j, kk] = c_start <= kc < c_start + WIN_W
                dc[mt, j, kk] = min(max(kc - qc, -(WIN_W - 1)), WIN_W - 1) + WIN_W - 1
    return dr, rv, dc, cv


KEY_ROWS_PER_LANE_TILE = LANES // K_COLS


def _bias_kernel(row_index, e_ref, o_ref):
    for typ in range(row_index.shape[0]):
        for mt in range(2):
            for i in range(Q_ROWS):
                for c in range(K_ROWS // KEY_ROWS_PER_LANE_TILE):
                    tile = None
                    for s in range(KEY_ROWS_PER_LANE_TILE):
                        piece = e_ref[0, int(row_index[typ, i, KEY_ROWS_PER_LANE_TILE * c + s]), mt, s]
                        tile = piece if tile is None else tile + piece
                    o_ref[0, 2 * typ + mt, pl.ds(Q_COLS * i, Q_COLS), pl.ds(LANES * c, LANES)] = tile


def _attention_bias(rpb):
    dr, rv, dc, cv = _attention_index_tables()
    nl, nh, n_dr, _ = rpb.shape
    e = jnp.where(jnp.asarray(cv)[None, None, None], rpb[:, :, :, dc].astype(F32), NEG_INF)
    e = jnp.concatenate([e, jnp.full((nl, nh, 1) + e.shape[3:], NEG_INF, F32)], axis=2)
    slots = [jnp.pad(e, ((0, 0),) * 5 + ((K_COLS * s, LANES - K_COLS * (s + 1)),))
             for s in range(KEY_ROWS_PER_LANE_TILE)]
    e4 = jnp.stack(slots, axis=4).reshape(nl * nh, n_dr + 1, 2, KEY_ROWS_PER_LANE_TILE, Q_COLS, LANES)
    row_index = np.where(rv, dr, n_dr)
    nq, nk = Q_ROWS * Q_COLS, K_ROWS * K_COLS
    return pl.pallas_call(
        functools.partial(_bias_kernel, row_index),
        grid=(nl * nh,),
        in_specs=[pl.BlockSpec((1,) + e4.shape[1:], lambda i: (i, 0, 0, 0, 0, 0))],
        out_specs=pl.BlockSpec((1, 6, nq, nk), lambda i: (i, 0, 0, 0)),
        out_shape=jax.ShapeDtypeStruct((nl * nh, 6, nq, nk), F32),
        compiler_params=_params(),
        name="attention_bias",
    )(e4)


def _pool_counts(seq):
    t = np.arange(seq)
    cols = []
    for w in POOL_WINDOWS:
        cnt = np.minimum(t + w // 2, seq) - np.maximum(t - w // 2, 0)
        cols.append(np.broadcast_to(cnt[:, None].astype(np.float32), (seq, POOL_GROUP_DIM)))
    return np.concatenate(cols, axis=1).reshape(seq, N_COL_BLOCKS, COL_BLOCK).transpose(1, 0, 2)


def _window_sum(ue, window):
    n = ue.shape[0]

    def shifted(x, d):
        return pltpu.roll(x, d % n, axis=0)

    s = ue + shifted(ue, 1)
    reach = 1
    while 2 * reach < window:
        s = shifted(s, reach) + shifted(s, -reach)
        reach *= 2
    return s


def _mix_kernel(q_ref, k_ref, v_ref, u_ref, bias_ref, cnt_ref, pw_ref, ps_ref, a_ref, p_ref):
    jb = pl.program_id(0)
    seq = q_ref.shape[1]
    nq = Q_ROWS * Q_COLS
    nk = K_ROWS * K_COLS
    lane_head = lax.broadcasted_iota(I32, (nq, COL_BLOCK), 1) // HEAD_DIM

    def rows(ref, grid_row, col, n):
        return ref[0, pl.ds(pl.multiple_of(grid_row * GRID_W + col, Q_COLS), n), :]

    def row_block(rb, carry):
        key_row0 = jnp.clip(Q_ROWS * rb - WIN_H // 2, 0, GRID_H - K_ROWS)
        typ = jnp.where(rb == 0, 0, jnp.where(rb == GRID_H // Q_ROWS - 1, 2, 1))
        outs = []
        for m in range(GRID_W // Q_COLS):
            mt = 0 if m < 3 else 1
            qb = jnp.concatenate([rows(q_ref, Q_ROWS * rb + i, Q_COLS * m, Q_COLS) for i in range(Q_ROWS)], axis=0)
            segs = _key_col_segments(m)
            kb = jnp.concatenate(
                [rows(k_ref, key_row0 + kr, c0, n) for kr in range(K_ROWS) for c0, n in segs], axis=0)
            vb = jnp.concatenate(
                [rows(v_ref, key_row0 + kr, c0, n) for kr in range(K_ROWS) for c0, n in segs], axis=0)
            zero = jnp.zeros_like(qb)
            qh = jnp.concatenate([jnp.where(lane_head == h, qb, zero) for h in range(HEADS_PER_BLOCK)], axis=0)
            s = lax.dot_general(qh, kb, (((1,), (1,)), ((), ())), preferred_element_type=F32)
            s = s + jnp.concatenate([bias_ref[h, 2 * typ + mt] for h in range(HEADS_PER_BLOCK)], axis=0)
            mx = jnp.max(s, axis=-1, keepdims=True)
            e = jnp.exp(s - mx)
            den = jnp.sum(e, axis=-1, keepdims=True)
            oh = jnp.dot(e.astype(BF16), vb, preferred_element_type=F32)
            o = oh[:nq]
            dn = jnp.broadcast_to(den[:nq], (nq, COL_BLOCK))
            for h in range(1, HEADS_PER_BLOCK):
                o = jnp.where(lane_head == h, oh[h * nq:(h + 1) * nq], o)
                dn = jnp.where(lane_head == h, den[h * nq:(h + 1) * nq], dn)
            outs.append(o / dn)
        for i in range(Q_ROWS):
            pieces = []
            for m in range(GRID_W // Q_COLS):
                pieces.append(outs[(m - 1) % 4][Q_COLS * i + Q_ROT:Q_COLS * (i + 1)])
                pieces.append(outs[m][Q_COLS * i:Q_COLS * i + Q_ROT])
            start = pl.multiple_of((Q_ROWS * rb + i) * GRID_W, GRID_W)
            a_ref[0, pl.ds(start, GRID_W), :] = jnp.concatenate(pieces, axis=0).astype(BF16)
        return carry

    lax.fori_loop(0, GRID_H // Q_ROWS, row_block, 0)

    def pool(windows):
        u = u_ref[0].astype(F32)
        pad = jnp.zeros((POOL_HALO, POOL_GROUP_DIM), F32)
        for gi, window in enumerate(windows):
            sl = slice(gi * POOL_GROUP_DIM, (gi + 1) * POOL_GROUP_DIM)
            ug = u[:, sl]
            win = _window_sum(jnp.concatenate([pad, ug, pad], axis=0), window)[POOL_HALO:POOL_HALO + seq]
            pooled = win / cnt_ref[0, :, sl] - ug
            mixed = jnp.dot(pooled.astype(BF16), pw_ref[0, gi].astype(BF16), preferred_element_type=F32) * ps_ref[0, gi]
            p_ref[0, :, sl] = mixed.astype(BF16)

    for cb in range(N_COL_BLOCKS):
        @pl.when(jb == cb)
        def _(cb=cb):
            pool(POOL_WINDOWS[cb * GROUPS_PER_BLOCK:(cb + 1) * GROUPS_PER_BLOCK])


def _mixer(proj, bias, cnt, pool_w, pool_scale, layer, batch, seq):
    proj3 = proj.reshape(batch, seq, D_IN)
    col = lambda base: (lambda j, b: (b, 0, base + j))
    blk = (1, seq, COL_BLOCK)
    bias_blk = (HEADS_PER_BLOCK,) + bias.shape[1:]
    a, p = pl.pallas_call(
        _mix_kernel,
        grid=(N_COL_BLOCKS, batch),
        in_specs=[
            pl.BlockSpec(blk, col(0)),
            pl.BlockSpec(blk, col(N_COL_BLOCKS)),
            pl.BlockSpec(blk, col(2 * N_COL_BLOCKS)),
            pl.BlockSpec(blk, col(3 * N_COL_BLOCKS)),
            pl.BlockSpec(bias_blk, lambda j, b: (layer * N_COL_BLOCKS + j, 0, 0, 0)),
            pl.BlockSpec(blk, lambda j, b: (j, 0, 0)),
            pl.BlockSpec((1, GROUPS_PER_BLOCK, POOL_GROUP_DIM, POOL_GROUP_DIM), lambda j, b: (layer, j, 0, 0)),
            pl.BlockSpec((1, GROUPS_PER_BLOCK, 1, POOL_GROUP_DIM), lambda j, b: (layer, j, 0, 0)),
        ],
        out_specs=[pl.BlockSpec(blk, lambda j, b: (b, 0, j)), pl.BlockSpec(blk, lambda j, b: (b, 0, j))],
        out_shape=[jax.ShapeDtypeStruct((batch, seq, D_ATTN), BF16), jax.ShapeDtypeStruct((batch, seq, D_POOL), BF16)],
        compiler_params=_params(2),
        name="mixer",
    )(proj3, proj3, proj3, proj3, bias, cnt, pool_w, pool_scale)
    return a.reshape(batch * seq, D_ATTN), p.reshape(batch * seq, D_POOL)


ROUTER_ROWS = 32
OUT_SPLIT = 2


def _first_max(vals):
    best = vals[0]
    for v in vals[1:]:
        best = jnp.maximum(best, v)
    idx = jnp.float32(len(vals) - 1)
    for i in range(len(vals) - 2, -1, -1):
        idx = jnp.where(vals[i] == best, jnp.float32(i), idx)
    return best, idx


def _route(lg):
    n = lg.shape[1]
    big = jnp.float32(-3e38)
    g = [lg[i:i + 1] for i in range(N_GROUPS)]
    gmax, gidx = _first_max(g)
    gsum = jnp.exp(g[0] - gmax)
    for v in g[1:]:
        gsum = gsum + jnp.exp(v - gmax)
    gw = 1.0 / gsum
    e = []
    for j in range(EXPERTS_PER_GROUP):
        v = lg[N_GROUPS + j:N_GROUPS + j + 1]
        for gg in range(1, N_GROUPS):
            row = N_GROUPS + EXPERTS_PER_GROUP * gg + j
            v = jnp.where(gidx == gg, lg[row:row + 1], v)
        e.append(v)
    v1, i1 = _first_max(e)
    v2, i2 = _first_max([jnp.where(i1 == j, big, e[j]) for j in range(EXPERTS_PER_GROUP)])
    d = jnp.exp(v2 - v1)
    w1 = gw * (1.0 / (1.0 + d))
    w2 = gw * (d / (1.0 + d))
    first_lo = i1 < i2
    lo = jnp.where(first_lo, i1, i2)
    hi = jnp.where(first_lo, i2, i1)
    w_lo = jnp.where(first_lo, w1, w2)
    w_hi = jnp.where(first_lo, w2, w1)
    cls = 6.0 * gidx + 0.5 * (lo * (7.0 - lo)) + (hi - lo - 1.0)
    return jnp.concatenate([w_lo, w_hi, cls, jnp.zeros((SUBLANES - 3, n), F32)], axis=0)


def _out_kernel(a_ref, p_ref, x_ref, wa_ref, wp_ref, g_ref, wr_ref, x2_ref, h_ref, r_ref, wa_bf, wp_bf):
    @pl.when(pl.program_id(0) == 0)
    def _():
        wa_bf[...] = wa_ref[0].astype(BF16)
        wp_bf[...] = wp_ref[0].astype(BF16)

    tm = x_ref.shape[0]
    n = tm // OUT_SPLIT
    nt_dims = (((1,), (1,)), ((), ()))
    for c in range(OUT_SPLIT):
        sl = pl.ds(c * n, n)
        mix = jnp.dot(a_ref[sl, :], wa_bf[...], preferred_element_type=F32)
        mix = mix + jnp.dot(p_ref[sl, :], wp_bf[...], preferred_element_type=F32)
        x2 = x_ref[sl, :] + mix
        x2_ref[sl, :] = x2
        h = (x2 * _rms_scale(x2)) * g_ref[0]
        _store_token_tiles(h_ref.at[pl.ds(c * n * TOKEN_ROWS, n * TOKEN_ROWS)], h)
        hh = h.astype(BF16)
        hl = (h - hh.astype(F32)).astype(BF16)
        r1 = lax.dot_general(wr_ref[0], hh, nt_dims, preferred_element_type=F32)
        r2 = lax.dot_general(wr_ref[0, :ROUTER_ROWS], hl, nt_dims, preferred_element_type=F32)
        r_ref[0, :, sl] = _route(r1[:ROUTER_ROWS] + r1[ROUTER_ROWS:] + r2)


def _out_proj(a, p, x, w_out, g, wr, layer):
    t = x.shape[0]
    tm = TM_DENSE
    row = lambda i: (i, 0)
    fixed = lambda i: (layer, 0, 0)
    return pl.pallas_call(
        _out_kernel,
        grid=(t // tm,),
        in_specs=[
            pl.BlockSpec((tm, D_ATTN), row),
            pl.BlockSpec((tm, D_POOL), row),
            pl.BlockSpec((tm, D_MODEL), row),
            pl.BlockSpec((1, D_ATTN, D_MODEL), fixed),
            pl.BlockSpec((1, D_POOL, D_MODEL), lambda i: (layer, 1, 0)),
            pl.BlockSpec((1, 1, D_MODEL), fixed),
            pl.BlockSpec((1, 2 * ROUTER_ROWS, D_MODEL), fixed),
        ],
        out_specs=[pl.BlockSpec((tm, D_MODEL), row),
                   pl.BlockSpec((tm * TOKEN_ROWS, LANES), row),
                   pl.BlockSpec((1, SUBLANES, tm), lambda i: (i, 0, 0))],
        out_shape=[jax.ShapeDtypeStruct((t, D_MODEL), F32),
                   jax.ShapeDtypeStruct((t * TOKEN_ROWS, LANES), F32),
                   jax.ShapeDtypeStruct((t // tm, SUBLANES, tm), F32)],
        scratch_shapes=[pltpu.VMEM((D_ATTN, D_MODEL), BF16), pltpu.VMEM((D_POOL, D_MODEL), BF16)],
        compiler_params=_params(),
        name="out_proj",
    )(a, p, x, w_out, w_out, g, wr)


def _moe_kernel(tok_ref, tile_ref, first_ref, last_ref, lo_ref, hi_ref, newc_ref, elo_ref, ehi_ref,
                h_hbm, w_ref, wg_lo, wu_lo, wd_lo, wg_hi, wu_hi, wd_hi, y_hbm,
                xbuf, ybuf, acc, wg_bf, wu_bf, wd_bf, gsem, ssem):
    del elo_ref, ehi_ref
    k = pl.program_id(0)
    tm = acc.shape[0]
    n_tiles = y_hbm.shape[0] // (tm * TOKEN_ROWS)
    tile = tile_ref[k]
    slot = tile % 2

    @pl.when(newc_ref[k] == 1)
    def _():
        for e, (wg, wu, wd) in enumerate(((wg_lo, wu_lo, wd_lo), (wg_hi, wu_hi, wd_hi))):
            wg_bf[e] = wg[0].astype(BF16)
            wu_bf[e] = wu[0].astype(BF16)
            wd_bf[e] = wd[0].astype(BF16)

    def token_rows(idx):
        return pl.ds(pl.multiple_of(idx * TOKEN_ROWS, TOKEN_ROWS), TOKEN_ROWS)

    def gather_copy(j, tok, sl):
        return pltpu.make_async_copy(h_hbm.at[token_rows(tok)], xbuf.at[sl, token_rows(j)], gsem.at[sl])

    def scatter_copy(j, tok, sl):
        return pltpu.make_async_copy(ybuf.at[sl, token_rows(j)], y_hbm.at[token_rows(tok)], ssem.at[sl])

    def start_rows(copy, tl, sl):
        def body(j, c):
            copy(j, tok_ref[tl * tm + j], sl).start()
            return c
        lax.fori_loop(0, tm, body, 0, unroll=8)

    def wait_rows(copy, sl):
        for j in range(tm):
            copy(j, 0, sl).wait()

    @pl.when(first_ref[k] == 1)
    def _():
        @pl.when(k == 0)
        def _():
            start_rows(gather_copy, tile, slot)
        wait_rows(gather_copy, slot)

        @pl.when(tile + 1 < n_tiles)
        def _():
            start_rows(gather_copy, tile + 1, 1 - slot)

        @pl.when(tile >= 2)
        def _():
            wait_rows(scatter_copy, slot)

    lo = lo_ref[k]
    hi = hi_ref[k]

    @pl.when(hi > lo)
    def _():
        x = _load_token_tiles(xbuf.at[slot], tm).astype(BF16)
        row = lax.broadcasted_iota(I32, (tm, 1), 0)
        mine = (row >= lo) & (row < hi)
        w = w_ref[...]
        w_lo = jnp.where(mine, w[:, 0:1], 0.0)
        w_hi = jnp.where(mine, w[:, 1:2], 0.0)

        def expert(e):
            a = jax.nn.silu(jnp.dot(x, wg_bf[e], preferred_element_type=F32)) * jnp.dot(x, wu_bf[e], preferred_element_type=F32)
            return jnp.dot(a.astype(BF16), wd_bf[e], preferred_element_type=F32)

        y = w_lo * expert(0) + w_hi * expert(1)

        @pl.when(lo == 0)
        def _():
            acc[...] = y

        @pl.when(lo > 0)
        def _():
            acc[...] += y

    @pl.when(last_ref[k] == 1)
    def _():
        _store_token_tiles(ybuf.at[slot], acc[...])
        start_rows(scatter_copy, tile, slot)

    @pl.when(k == pl.num_programs(0) - 1)
    def _():
        wait_rows(scatter_copy, slot)
        if n_tiles > 1:
            wait_rows(scatter_copy, 1 - slot)


def _moe_tables(routing, n_tok, tm, expert0):
    w_lo = routing[:, 0, :].reshape(n_tok)
    w_hi = routing[:, 1, :].reshape(n_tok)
    cls = routing[:, 2, :].reshape(n_tok).astype(I32)
    key = cls * n_tok + jnp.arange(n_tok, dtype=I32)
    skey, w_lo, w_hi = lax.sort((key, w_lo, w_hi), num_keys=1)
    tok_sorted = skey % n_tok
    bounds = jnp.arange(N_CLASSES + 1, dtype=I32) * n_tok
    cstart = jnp.sum((skey[None, :] < bounds[:, None]).astype(I32), axis=1)
    n_tiles = n_tok // tm
    cuts = jnp.sort(jnp.concatenate([jnp.arange(n_tiles, dtype=I32) * tm, cstart[1:N_CLASSES]]))
    ends = jnp.concatenate([cuts[1:], jnp.full((1,), n_tok, I32)])
    tile = jnp.minimum(cuts // tm, n_tiles - 1)
    c = jnp.clip(jnp.sum((cstart[None, :] <= cuts[:, None]).astype(I32), axis=1) - 1, 0, N_CLASSES - 1)
    lo = cuts - tile * tm
    hi = ends - tile * tm
    change = (tile[1:] != tile[:-1]).astype(I32)
    one = jnp.ones((1,), I32)
    first = jnp.concatenate([one, change])
    last = jnp.concatenate([change, one])
    new_class = jnp.concatenate([one, (c[1:] != c[:-1]).astype(I32)])
    grp = c // 6
    e_lo = expert0 + EXPERTS_PER_GROUP * grp + jnp.asarray(PAIR_LO)[c % 6]
    e_hi = expert0 + EXPERTS_PER_GROUP * grp + jnp.asarray(PAIR_HI)[c % 6]
    scalars = tuple(v.astype(I32) for v in (tok_sorted, tile, first, last, lo, hi, new_class, e_lo, e_hi))
    return scalars, jnp.stack([w_lo, w_hi], axis=1)


def _moe(h_tiles, routing, w_gate, w_up, w_down, layer):
    n_tok = h_tiles.shape[0] // TOKEN_ROWS
    tm = TM_MOE
    scalars, w_pair = _moe_tables(routing, n_tok, tm, layer * N_GROUPS * EXPERTS_PER_GROUP)
    n_steps = scalars[1].shape[0]
    by_tile = lambda k, tok, tile, first, last, lo, hi, newc, elo, ehi: (tile[k], 0)
    e_lo = lambda k, tok, tile, first, last, lo, hi, newc, elo, ehi: (elo[k], 0, 0)
    e_hi = lambda k, tok, tile, first, last, lo, hi, newc, elo, ehi: (ehi[k], 0, 0)
    up_blk = (1, D_MODEL, D_EXPERT)
    down_blk = (1, D_EXPERT, D_MODEL)
    grid_spec = pltpu.PrefetchScalarGridSpec(
        num_scalar_prefetch=len(scalars),
        grid=(n_steps,),
        in_specs=[
            pl.BlockSpec(memory_space=pl.ANY),
            pl.BlockSpec((tm, 2), by_tile),
            pl.BlockSpec(up_blk, e_lo), pl.BlockSpec(up_blk, e_lo), pl.BlockSpec(down_blk, e_lo),
            pl.BlockSpec(up_blk, e_hi), pl.BlockSpec(up_blk, e_hi), pl.BlockSpec(down_blk, e_hi),
        ],
        out_specs=pl.BlockSpec(memory_space=pl.ANY),
        scratch_shapes=[
            pltpu.VMEM((2, tm * TOKEN_ROWS, LANES), F32),
            pltpu.VMEM((2, tm * TOKEN_ROWS, LANES), F32),
            pltpu.VMEM((tm, D_MODEL), F32),
            pltpu.VMEM((2, D_MODEL, D_EXPERT), BF16),
            pltpu.VMEM((2, D_MODEL, D_EXPERT), BF16),
            pltpu.VMEM((2, D_EXPERT, D_MODEL), BF16),
            pltpu.SemaphoreType.DMA((2,)),
            pltpu.SemaphoreType.DMA((2,)),
        ],
    )
    return pl.pallas_call(
        _moe_kernel,
        grid_spec=grid_spec,
        out_shape=jax.ShapeDtypeStruct((n_tok * TOKEN_ROWS, LANES), F32),
        compiler_params=_params(),
        name="moe",
    )(*scalars, h_tiles, w_pair, w_gate, w_up, w_down, w_gate, w_up, w_down)


def _final_kernel(x_ref, y_ref, g_ref, o_ref):
    x = x_ref[...] + _load_token_tiles(y_ref, x_ref.shape[0])
    o_ref[...] = (x * _rms_scale(x)) * g_ref[...]


def _final_norm(x, y, g):
    t = x.shape[0]
    tm = TM_DENSE
    row = lambda i: (i, 0)
    return pl.pallas_call(
        _final_kernel,
        grid=(t // tm,),
        in_specs=[pl.BlockSpec((tm, D_MODEL), row), pl.BlockSpec((tm * TOKEN_ROWS, LANES), row),
                  pl.BlockSpec((1, D_MODEL), lambda i: (0, 0))],
        out_specs=pl.BlockSpec((tm, D_MODEL), row),
        out_shape=jax.ShapeDtypeStruct((t, D_MODEL), F32),
        compiler_params=_params(),
        name="final_norm",
    )(x, y, g)


def _split_bf16(w):
    hi = w.astype(BF16)
    lo = (w - hi.astype(F32)).astype(BF16)
    return hi, lo


def kernel(x, norm_mix_g, w_in, rpb, pool_w, pool_scale, w_out, norm_ffn_g,
           w_router_group, w_router_expert, w_gate, w_up, w_down, final_g):
    batch, seq, d = x.shape
    depth = w_in.shape[0]
    assert d == D_MODEL and seq == GRID_H * GRID_W
    n_tok = batch * seq
    assert n_tok % TM_DENSE == 0 and n_tok % TM_MOE == 0

    bias = _attention_bias(rpb)
    cnt = jnp.asarray(_pool_counts(seq))
    n_pool = D_POOL // POOL_GROUP_DIM
    pool_scale4 = pool_scale.reshape(depth, n_pool, 1, POOL_GROUP_DIM)
    g_mix = norm_mix_g.reshape(depth, 1, D_MODEL)
    g_ffn = norm_ffn_g.reshape(depth, 1, D_MODEL)
    n_exp = N_GROUPS * EXPERTS_PER_GROUP
    w_gate_s = w_gate.reshape(depth * n_exp, D_MODEL, D_EXPERT)
    w_up_s = w_up.reshape(depth * n_exp, D_MODEL, D_EXPERT)
    w_down_s = w_down.reshape(depth * n_exp, D_EXPERT, D_MODEL)
    w_router = jnp.concatenate([w_router_group, w_router_expert], axis=-1).transpose(0, 2, 1)
    w_router = jnp.pad(w_router, ((0, 0), (0, ROUTER_ROWS - w_router.shape[1]), (0, 0)))
    w_router_b = jnp.concatenate(_split_bf16(w_router), axis=1)

    xf = x.reshape(n_tok, D_MODEL)
    y = None
    for l in range(depth):
        xf, proj = _in_proj(xf, y, g_mix, w_in, l)
        a, p = _mixer(proj, bias, cnt, pool_w, pool_scale4, l, batch, seq)
        xf, h_tiles, routing = _out_proj(a, p, xf, w_out, g_ffn, w_router_b, l)
        y = _moe(h_tiles, routing, w_gate_s, w_up_s, w_down_s, l)
    out = _final_norm(xf, y, final_g.reshape(1, D_MODEL))
    return out.reshape(batch, seq, D_MODEL)
```

```python
import functools

import numpy as np
import jax
import jax.numpy as jnp
from jax import lax
from jax.experimental import pallas as pl
from jax.experimental.pallas import tpu as pltpu

F32 = jnp.float32
BF16 = jnp.bfloat16
I32 = jnp.int32

D_MODEL = 1024
GRID_W = 64
GRID_H = 32
D_ATTN = 512
HEAD_DIM = 64
N_HEADS = 8
WIN_H = 8
WIN_W = 16
D_POOL = 512
POOL_GROUP_DIM = 128
D_IN = 2048
N_GROUPS = 4
EXPERTS_PER_GROUP = 4
D_EXPERT = 512
N_CLASSES = 24
EPS = 1e-6
NEG_INF = -1e30

LANES = 128
SUBLANES = 8
TOKEN_ROWS = D_MODEL // LANES
COL_BLOCK = 256
HEADS_PER_BLOCK = COL_BLOCK // HEAD_DIM
GROUPS_PER_BLOCK = COL_BLOCK // POOL_GROUP_DIM
N_COL_BLOCKS = D_ATTN // COL_BLOCK
Q_ROWS = 8
Q_COLS = 16
K_ROWS = 16
K_COLS = 32
Q_ROT = 8
ROW_BLOCK_KEY_START = (0, 4, 12, 16)
POOL_HALO = 16
POOL_WINDOWS = (2, 4, 8, 16)

TM_DENSE = 512
TM_MOE = 256
VMEM_LIMIT = 56 * 1024 * 1024

PAIR_LO = np.array([0, 0, 0, 1, 1, 2], np.int32)
PAIR_HI = np.array([1, 2, 3, 2, 3, 3], np.int32)


def _params(n_axes=1):
    return pltpu.CompilerParams(dimension_semantics=("arbitrary",) * n_axes, vmem_limit_bytes=VMEM_LIMIT)


def _rms_scale(x):
    return lax.rsqrt(jnp.mean(x * x, axis=-1, keepdims=True) + EPS)


def _load_token_tiles(ref, n_tok):
    return jnp.concatenate([ref[pl.ds(s, n_tok, stride=TOKEN_ROWS), :] for s in range(TOKEN_ROWS)], axis=1)


def _store_token_tiles(ref, val):
    n_tok = val.shape[0]
    for s in range(TOKEN_ROWS):
        ref[pl.ds(s, n_tok, stride=TOKEN_ROWS), :] = val[:, s * LANES:(s + 1) * LANES]


def _x_spec(tm, tiled):
    shape = (tm * TOKEN_ROWS, LANES) if tiled else (tm, D_MODEL)
    return pl.BlockSpec(shape, lambda i: (i, 0))


def _in_kernel(x_tiled, x_ref, g_ref, w_ref, proj_ref, w_bf):
    @pl.when(pl.program_id(0) == 0)
    def _():
        w_bf[...] = w_ref[0].astype(BF16)

    tm = proj_ref.shape[0]
    x = _load_token_tiles(x_ref, tm) if x_tiled else x_ref[...]
    h = ((x * _rms_scale(x)) * g_ref[0]).astype(BF16)
    proj = jnp.dot(h, w_bf[...], preferred_element_type=F32)
    q = (proj[:, :D_ATTN] * (HEAD_DIM ** -0.5)).reshape(tm // GRID_W, GRID_W, D_ATTN)
    q = jnp.concatenate([q[:, Q_ROT:], q[:, :Q_ROT]], axis=1).reshape(tm, D_ATTN)
    proj_ref[:, :D_ATTN] = q.astype(BF16)
    proj_ref[:, D_ATTN:] = proj[:, D_ATTN:].astype(BF16)


def _in_proj(x, x_tiled, g, w, layer, n_tok):
    tm = TM_DENSE
    fixed = lambda i: (layer, 0, 0)
    return pl.pallas_call(
        functools.partial(_in_kernel, x_tiled),
        grid=(n_tok // tm,),
        in_specs=[_x_spec(tm, x_tiled), pl.BlockSpec((1, 1, D_MODEL), fixed), pl.BlockSpec((1, D_MODEL, D_IN), fixed)],
        out_specs=pl.BlockSpec((tm, D_IN), lambda i: (i, 0)),
        out_shape=jax.ShapeDtypeStruct((n_tok, D_IN), BF16),
        scratch_shapes=[pltpu.VMEM((D_MODEL, D_IN), BF16)],
        compiler_params=_params(),
        name="in_proj",
    )(x, g, w)


def _key_col_segments(m):
    if m < 3:
        return ((Q_COLS * m, K_COLS),)
    return ((0, K_COLS // 2), (GRID_W - K_COLS // 2, K_COLS // 2))


def _key_col(m, kk):
    for start, n in _key_col_segments(m):
        if kk < n:
            return start + kk
        kk -= n
    raise ValueError(kk)


def _attention_index_tables():
    dr = np.zeros((3, Q_ROWS, K_ROWS), np.int32)
    rv = np.zeros((3, Q_ROWS, K_ROWS), bool)
    for typ, rb in enumerate((0, 1, 3)):
        for i in range(Q_ROWS):
            r = Q_ROWS * rb + i
            r_start = min(max(r - WIN_H // 2, 0), GRID_H - WIN_H)
            for kr in range(K_ROWS):
                ka = ROW_BLOCK_KEY_START[rb] + kr
                rv[typ, i, kr] = r_start <= ka < r_start + WIN_H
                dr[typ, i, kr] = min(max(ka - r + WIN_H - 1, 0), 2 * WIN_H - 2)
    dc = np.zeros((2, Q_COLS, K_COLS), np.int32)
    cv = np.zeros((2, Q_COLS, K_COLS), bool)
    for mt, m in enumerate((0, 3)):
        for j in range(Q_COLS):
            qc = (Q_COLS * m + j + Q_ROT) % GRID_W
            c_start = min(max(qc - WIN_W // 2, 0), GRID_W - WIN_W)
            for kk in range(K_COLS):
                kc = _key_col(m, kk)
                cv[mt, j, kk] = c_start <= kc < c_start + WIN_W
                dc[mt, j, kk] = min(max(kc - qc, -(WIN_W - 1)), WIN_W - 1) + WIN_W - 1
    return dr, rv, dc, cv


KEY_ROWS_PER_LANE_TILE = LANES // K_COLS


def _bias_kernel(row_index, e_ref, o_ref):
    for typ in range(row_index.shape[0]):
        for mt in range(2):
            for i in range(Q_ROWS):
                for c in range(K_ROWS // KEY_ROWS_PER_LANE_TILE):
                    tile = None
                    for s in range(KEY_ROWS_PER_LANE_TILE):
                        piece = e_ref[0, int(row_index[typ, i, KEY_ROWS_PER_LANE_TILE * c + s]), mt, s]
                        tile = piece if tile is None else tile + piece
                    o_ref[0, 2 * typ + mt, pl.ds(Q_COLS * i, Q_COLS), pl.ds(LANES * c, LANES)] = tile


def _attention_bias(rpb):
    dr, rv, dc, cv = _attention_index_tables()
    nl, nh, n_dr, _ = rpb.shape
    e = jnp.where(jnp.asarray(cv)[None, None, None], rpb[:, :, :, dc].astype(F32), NEG_INF)
    e = jnp.concatenate([e, jnp.full((nl, nh, 1) + e.shape[3:], NEG_INF, F32)], axis=2)
    slots = [jnp.pad(e, ((0, 0),) * 5 + ((K_COLS * s, LANES - K_COLS * (s + 1)),))
             for s in range(KEY_ROWS_PER_LANE_TILE)]
    e4 = jnp.stack(slots, axis=4).reshape(nl * nh, n_dr + 1, 2, KEY_ROWS_PER_LANE_TILE, Q_COLS, LANES)
    row_index = np.where(rv, dr, n_dr)
    nq, nk = Q_ROWS * Q_COLS, K_ROWS * K_COLS
    return pl.pallas_call(
        functools.partial(_bias_kernel, row_index),
        grid=(nl * nh,),
        in_specs=[pl.BlockSpec((1,) + e4.shape[1:], lambda i: (i, 0, 0, 0, 0, 0))],
        out_specs=pl.BlockSpec((1, 6, nq, nk), lambda i: (i, 0, 0, 0)),
        out_shape=jax.ShapeDtypeStruct((nl * nh, 6, nq, nk), F32),
        compiler_params=_params(),
        name="attention_bias",
    )(e4)


def _pool_counts(seq):
    t = np.arange(seq)
    cols = []
    for w in POOL_WINDOWS:
        cnt = np.minimum(t + w // 2, seq) - np.maximum(t - w // 2, 0)
        cols.append(np.broadcast_to(cnt[:, None].astype(np.float32), (seq, POOL_GROUP_DIM)))
    return np.concatenate(cols, axis=1).reshape(seq, N_COL_BLOCKS, COL_BLOCK).transpose(1, 0, 2)


def _window_sum(ue, window):
    n = ue.shape[0]

    def shifted(x, d):
        return pltpu.roll(x, d % n, axis=0)

    s = ue + shifted(ue, 1)
    reach = 1
    while 2 * reach < window:
        s = shifted(s, reach) + shifted(s, -reach)
        reach *= 2
    return s


def _mix_kernel(q_ref, k_ref, v_ref, u_ref, bias_ref, cnt_ref, pw_ref, ps_ref, a_ref, p_ref):
    jb = pl.program_id(0)
    seq = q_ref.shape[1]
    nq = Q_ROWS * Q_COLS
    nk = K_ROWS * K_COLS
    lane_head = lax.broadcasted_iota(I32, (nq, COL_BLOCK), 1) // HEAD_DIM

    def rows(ref, grid_row, col, n):
        return ref[0, pl.ds(pl.multiple_of(grid_row * GRID_W + col, Q_COLS), n), :]

    def row_block(rb, carry):
        key_row0 = jnp.clip(Q_ROWS * rb - WIN_H // 2, 0, GRID_H - K_ROWS)
        typ = jnp.where(rb == 0, 0, jnp.where(rb == GRID_H // Q_ROWS - 1, 2, 1))
        outs = []
        for m in range(GRID_W // Q_COLS):
            mt = 0 if m < 3 else 1
            qb = jnp.concatenate([rows(q_ref, Q_ROWS * rb + i, Q_COLS * m, Q_COLS) for i in range(Q_ROWS)], axis=0)
            segs = _key_col_segments(m)
            kb = jnp.concatenate(
                [rows(k_ref, key_row0 + kr, c0, n) for kr in range(K_ROWS) for c0, n in segs], axis=0)
            vb = jnp.concatenate(
                [rows(v_ref, key_row0 + kr, c0, n) for kr in range(K_ROWS) for c0, n in segs], axis=0)
            zero = jnp.zeros_like(qb)
            qh = jnp.concatenate([jnp.where(lane_head == h, qb, zero) for h in range(HEADS_PER_BLOCK)], axis=0)
            s = lax.dot_general(qh, kb, (((1,), (1,)), ((), ())), preferred_element_type=F32)
            s = s + jnp.concatenate([bias_ref[h, 2 * typ + mt] for h in range(HEADS_PER_BLOCK)], axis=0)
            mx = jnp.max(s, axis=-1, keepdims=True)
            e = jnp.exp(s - mx)
            den = jnp.sum(e, axis=-1, keepdims=True)
            oh = jnp.dot(e.astype(BF16), vb, preferred_element_type=F32)
            o = oh[:nq]
            dn = jnp.broadcast_to(den[:nq], (nq, COL_BLOCK))
            for h in range(1, HEADS_PER_BLOCK):
                o = jnp.where(lane_head == h, oh[h * nq:(h + 1) * nq], o)
                dn = jnp.where(lane_head == h, den[h * nq:(h + 1) * nq], dn)
            outs.append(o / dn)
        for i in range(Q_ROWS):
            pieces = []
            for m in range(GRID_W // Q_COLS):
                pieces.append(outs[(m - 1) % 4][Q_COLS * i + Q_ROT:Q_COLS * (i + 1)])
                pieces.append(outs[m][Q_COLS * i:Q_COLS * i + Q_ROT])
            start = pl.multiple_of((Q_ROWS * rb + i) * GRID_W, GRID_W)
            a_ref[0, pl.ds(start, GRID_W), :] = jnp.concatenate(pieces, axis=0).astype(BF16)
        return carry

    lax.fori_loop(0, GRID_H // Q_ROWS, row_block, 0)

    def pool(windows):
        u = u_ref[0].astype(F32)
        pad = jnp.zeros((POOL_HALO, POOL_GROUP_DIM), F32)
        for gi, window in enumerate(windows):
            sl = slice(gi * POOL_GROUP_DIM, (gi + 1) * POOL_GROUP_DIM)
            ug = u[:, sl]
            win = _window_sum(jnp.concatenate([pad, ug, pad], axis=0), window)[POOL_HALO:POOL_HALO + seq]
            pooled = win / cnt_ref[0, :, sl] - ug
            mixed = jnp.dot(pooled.astype(BF16), pw_ref[0, gi].astype(BF16), preferred_element_type=F32) * ps_ref[0, gi]
            p_ref[0, :, sl] = mixed.astype(BF16)

    for cb in range(N_COL_BLOCKS):
        @pl.when(jb == cb)
        def _(cb=cb):
            pool(POOL_WINDOWS[cb * GROUPS_PER_BLOCK:(cb + 1) * GROUPS_PER_BLOCK])


def _mixer(proj, bias, cnt, pool_w, pool_scale, layer, batch, seq):
    proj3 = proj.reshape(batch, seq, D_IN)
    col = lambda base: (lambda j, b: (b, 0, base + j))
    blk = (1, seq, COL_BLOCK)
    bias_blk = (HEADS_PER_BLOCK,) + bias.shape[1:]
    a, p = pl.pallas_call(
        _mix_kernel,
        grid=(N_COL_BLOCKS, batch),
        in_specs=[
            pl.BlockSpec(blk, col(0)),
            pl.BlockSpec(blk, col(N_COL_BLOCKS)),
            pl.BlockSpec(blk, col(2 * N_COL_BLOCKS)),
            pl.BlockSpec(blk, col(3 * N_COL_BLOCKS)),
            pl.BlockSpec(bias_blk, lambda j, b: (layer * N_COL_BLOCKS + j, 0, 0, 0)),
            pl.BlockSpec(blk, lambda j, b: (j, 0, 0)),
            pl.BlockSpec((1, GROUPS_PER_BLOCK, POOL_GROUP_DIM, POOL_GROUP_DIM), lambda j, b: (layer, j, 0, 0)),
            pl.BlockSpec((1, GROUPS_PER_BLOCK, 1, POOL_GROUP_DIM), lambda j, b: (layer, j, 0, 0)),
        ],
        out_specs=[pl.BlockSpec(blk, lambda j, b: (b, 0, j)), pl.BlockSpec(blk, lambda j, b: (b, 0, j))],
        out_shape=[jax.ShapeDtypeStruct((batch, seq, D_ATTN), BF16), jax.ShapeDtypeStruct((batch, seq, D_POOL), BF16)],
        compiler_params=_params(2),
        name="mixer",
    )(proj3, proj3, proj3, proj3, bias, cnt, pool_w, pool_scale)
    return a.reshape(batch * seq, D_ATTN), p.reshape(batch * seq, D_POOL)


ROUTER_ROWS = 32
OUT_SPLIT = 2


def _first_max(vals):
    best = vals[0]
    for v in vals[1:]:
        best = jnp.maximum(best, v)
    idx = jnp.float32(len(vals) - 1)
    for i in range(len(vals) - 2, -1, -1):
        idx = jnp.where(vals[i] == best, jnp.float32(i), idx)
    return best, idx


def _route(lg):
    n = lg.shape[1]
    big = jnp.float32(-3e38)
    g = [lg[i:i + 1] for i in range(N_GROUPS)]
    gmax, gidx = _first_max(g)
    gsum = jnp.exp(g[0] - gmax)
    for v in g[1:]:
        gsum = gsum + jnp.exp(v - gmax)
    gw = 1.0 / gsum
    e = []
    for j in range(EXPERTS_PER_GROUP):
        v = lg[N_GROUPS + j:N_GROUPS + j + 1]
        for gg in range(1, N_GROUPS):
            row = N_GROUPS + EXPERTS_PER_GROUP * gg + j
            v = jnp.where(gidx == gg, lg[row:row + 1], v)
        e.append(v)
    v1, i1 = _first_max(e)
    v2, i2 = _first_max([jnp.where(i1 == j, big, e[j]) for j in range(EXPERTS_PER_GROUP)])
    d = jnp.exp(v2 - v1)
    w1 = gw * (1.0 / (1.0 + d))
    w2 = gw * (d / (1.0 + d))
    first_lo = i1 < i2
    lo = jnp.where(first_lo, i1, i2)
    hi = jnp.where(first_lo, i2, i1)
    w_lo = jnp.where(first_lo, w1, w2)
    w_hi = jnp.where(first_lo, w2, w1)
    cls = 6.0 * gidx + 0.5 * (lo * (7.0 - lo)) + (hi - lo - 1.0)
    return jnp.concatenate([w_lo, w_hi, cls, jnp.zeros((SUBLANES - 3, n), F32)], axis=0)


def _out_kernel(x_tiled, a_ref, p_ref, x_ref, wa_ref, wp_ref, g_ref, wr_ref, x2_ref, r_ref, wa_bf, wp_bf):
    @pl.when(pl.program_id(0) == 0)
    def _():
        wa_bf[...] = wa_ref[0].astype(BF16)
        wp_bf[...] = wp_ref[0].astype(BF16)

    tm = a_ref.shape[0]
    n = tm // OUT_SPLIT
    nt_dims = (((1,), (1,)), ((), ()))
    for c in range(OUT_SPLIT):
        sl = pl.ds(c * n, n)
        tiles = pl.ds(c * n * TOKEN_ROWS, n * TOKEN_ROWS)
        mix = jnp.dot(a_ref[sl, :], wa_bf[...], preferred_element_type=F32)
        mix = mix + jnp.dot(p_ref[sl, :], wp_bf[...], preferred_element_type=F32)
        x2 = (_load_token_tiles(x_ref.at[tiles], n) if x_tiled else x_ref[sl, :]) + mix
        _store_token_tiles(x2_ref.at[tiles], x2)
        h = (x2 * _rms_scale(x2)) * g_ref[0]
        hh = h.astype(BF16)
        hl = (h - hh.astype(F32)).astype(BF16)
        r1 = lax.dot_general(wr_ref[0], hh, nt_dims, preferred_element_type=F32)
        r2 = lax.dot_general(wr_ref[0, :ROUTER_ROWS], hl, nt_dims, preferred_element_type=F32)
        r_ref[0, :, sl] = _route(r1[:ROUTER_ROWS] + r1[ROUTER_ROWS:] + r2)


def _out_proj(a, p, x, x_tiled, w_out, g, wr, layer):
    t = a.shape[0]
    tm = TM_DENSE
    row = lambda i: (i, 0)
    fixed = lambda i: (layer, 0, 0)
    return pl.pallas_call(
        functools.partial(_out_kernel, x_tiled),
        grid=(t // tm,),
        in_specs=[
            pl.BlockSpec((tm, D_ATTN), row),
            pl.BlockSpec((tm, D_POOL), row),
            _x_spec(tm, x_tiled),
            pl.BlockSpec((1, D_ATTN, D_MODEL), fixed),
            pl.BlockSpec((1, D_POOL, D_MODEL), lambda i: (layer, 1, 0)),
            pl.BlockSpec((1, 1, D_MODEL), fixed),
            pl.BlockSpec((1, 2 * ROUTER_ROWS, D_MODEL), fixed),
        ],
        out_specs=[_x_spec(tm, True),
                   pl.BlockSpec((1, SUBLANES, tm), lambda i: (i, 0, 0))],
        out_shape=[jax.ShapeDtypeStruct((t * TOKEN_ROWS, LANES), F32),
                   jax.ShapeDtypeStruct((t // tm, SUBLANES, tm), F32)],
        scratch_shapes=[pltpu.VMEM((D_ATTN, D_MODEL), BF16), pltpu.VMEM((D_POOL, D_MODEL), BF16)],
        compiler_params=_params(),
        name="out_proj",
    )(a, p, x, w_out, w_out, g, wr)


def _moe_kernel(tok_ref, tile_ref, first_ref, last_ref, lo_ref, hi_ref, newc_ref, elo_ref, ehi_ref,
                h_hbm, w_ref, g_ref, wg_lo, wu_lo, wd_lo, wg_hi, wu_hi, wd_hi, y_hbm,
                xbuf, ybuf, acc, wg_bf, wu_bf, wd_bf, gsem, ssem):
    del elo_ref, ehi_ref
    k = pl.program_id(0)
    tm = acc.shape[0]
    n_tiles = y_hbm.shape[0] // (tm * TOKEN_ROWS)
    tile = tile_ref[k]
    slot = tile % 2

    @pl.when(newc_ref[k] == 1)
    def _():
        for e, (wg, wu, wd) in enumerate(((wg_lo, wu_lo, wd_lo), (wg_hi, wu_hi, wd_hi))):
            wg_bf[e] = wg[0].astype(BF16)
            wu_bf[e] = wu[0].astype(BF16)
            wd_bf[e] = wd[0].astype(BF16)

    def token_rows(idx):
        return pl.ds(pl.multiple_of(idx * TOKEN_ROWS, TOKEN_ROWS), TOKEN_ROWS)

    def gather_copy(j, tok, sl):
        return pltpu.make_async_copy(h_hbm.at[token_rows(tok)], xbuf.at[sl, token_rows(j)], gsem.at[sl])

    def scatter_copy(j, tok, sl):
        return pltpu.make_async_copy(ybuf.at[sl, token_rows(j)], y_hbm.at[token_rows(tok)], ssem.at[sl])

    def start_rows(copy, tl, sl):
        def body(j, c):
            copy(j, tok_ref[tl * tm + j], sl).start()
            return c
        lax.fori_loop(0, tm, body, 0, unroll=8)

    def wait_rows(copy, sl):
        for j in range(tm):
            copy(j, 0, sl).wait()

    @pl.when(first_ref[k] == 1)
    def _():
        @pl.when(k == 0)
        def _():
            start_rows(gather_copy, tile, slot)
        wait_rows(gather_copy, slot)

        @pl.when(tile + 1 < n_tiles)
        def _():
            start_rows(gather_copy, tile + 1, 1 - slot)

        @pl.when(tile >= 2)
        def _():
            wait_rows(scatter_copy, slot)

    lo = lo_ref[k]
    hi = hi_ref[k]

    @pl.when(hi > lo)
    def _():
        xr = _load_token_tiles(xbuf.at[slot], tm)
        x = ((xr * _rms_scale(xr)) * g_ref[0]).astype(BF16)
        row = lax.broadcasted_iota(I32, (tm, 1), 0)
        mine = (row >= lo) & (row < hi)
        w = w_ref[...]
        w_lo = jnp.where(mine, w[:, 0:1], 0.0)
        w_hi = jnp.where(mine, w[:, 1:2], 0.0)

        def expert(e):
            a = jax.nn.silu(jnp.dot(x, wg_bf[e], preferred_element_type=F32)) * jnp.dot(x, wu_bf[e], preferred_element_type=F32)
            return jnp.dot(a.astype(BF16), wd_bf[e], preferred_element_type=F32)

        y = w_lo * expert(0) + w_hi * expert(1)

        @pl.when(lo == 0)
        def _():
            acc[...] = y

        @pl.when(lo > 0)
        def _():
            acc[...] += y

    @pl.when(last_ref[k] == 1)
    def _():
        xs = xbuf.at[slot]
        ys = ybuf.at[slot]
        for s in range(TOKEN_ROWS):
            rows_s = pl.ds(s, tm, stride=TOKEN_ROWS)
            ys[rows_s, :] = xs[rows_s, :] + acc[:, s * LANES:(s + 1) * LANES]
        start_rows(scatter_copy, tile, slot)

    @pl.when(k == pl.num_programs(0) - 1)
    def _():
        wait_rows(scatter_copy, slot)
        if n_tiles > 1:
            wait_rows(scatter_copy, 1 - slot)


def _moe_tables(routing, n_tok, tm, expert0):
    w_lo = routing[:, 0, :].reshape(n_tok)
    w_hi = routing[:, 1, :].reshape(n_tok)
    cls = routing[:, 2, :].reshape(n_tok).astype(I32)
    key = cls * n_tok + jnp.arange(n_tok, dtype=I32)
    skey, w_lo, w_hi = lax.sort((key, w_lo, w_hi), num_keys=1)
    tok_sorted = skey % n_tok
    bounds = jnp.arange(N_CLASSES + 1, dtype=I32) * n_tok
    cstart = jnp.sum((skey[None, :] < bounds[:, None]).astype(I32), axis=1)
    n_tiles = n_tok // tm
    cuts = jnp.sort(jnp.concatenate([jnp.arange(n_tiles, dtype=I32) * tm, cstart[1:N_CLASSES]]))
    ends = jnp.concatenate([cuts[1:], jnp.full((1,), n_tok, I32)])
    tile = jnp.minimum(cuts // tm, n_tiles - 1)
    c = jnp.clip(jnp.sum((cstart[None, :] <= cuts[:, None]).astype(I32), axis=1) - 1, 0, N_CLASSES - 1)
    lo = cuts - tile * tm
    hi = ends - tile * tm
    change = (tile[1:] != tile[:-1]).astype(I32)
    one = jnp.ones((1,), I32)
    first = jnp.concatenate([one, change])
    last = jnp.concatenate([change, one])
    new_class = jnp.concatenate([one, (c[1:] != c[:-1]).astype(I32)])
    grp = c // 6
    e_lo = expert0 + EXPERTS_PER_GROUP * grp + jnp.asarray(PAIR_LO)[c % 6]
    e_hi = expert0 + EXPERTS_PER_GROUP * grp + jnp.asarray(PAIR_HI)[c % 6]
    scalars = tuple(v.astype(I32) for v in (tok_sorted, tile, first, last, lo, hi, new_class, e_lo, e_hi))
    return scalars, jnp.stack([w_lo, w_hi], axis=1)


def _moe(x_tiles, routing, g, w_gate, w_up, w_down, layer):
    n_tok = x_tiles.shape[0] // TOKEN_ROWS
    tm = TM_MOE
    scalars, w_pair = _moe_tables(routing, n_tok, tm, layer * N_GROUPS * EXPERTS_PER_GROUP)
    n_steps = scalars[1].shape[0]
    by_tile = lambda k, tok, tile, first, last, lo, hi, newc, elo, ehi: (tile[k], 0)
    e_lo = lambda k, tok, tile, first, last, lo, hi, newc, elo, ehi: (elo[k], 0, 0)
    e_hi = lambda k, tok, tile, first, last, lo, hi, newc, elo, ehi: (ehi[k], 0, 0)
    up_blk = (1, D_MODEL, D_EXPERT)
    down_blk = (1, D_EXPERT, D_MODEL)
    grid_spec = pltpu.PrefetchScalarGridSpec(
        num_scalar_prefetch=len(scalars),
        grid=(n_steps,),
        in_specs=[
            pl.BlockSpec(memory_space=pl.ANY),
            pl.BlockSpec((tm, 2), by_tile),
            pl.BlockSpec((1, 1, D_MODEL), lambda k, *_: (layer, 0, 0)),
            pl.BlockSpec(up_blk, e_lo), pl.BlockSpec(up_blk, e_lo), pl.BlockSpec(down_blk, e_lo),
            pl.BlockSpec(up_blk, e_hi), pl.BlockSpec(up_blk, e_hi), pl.BlockSpec(down_blk, e_hi),
        ],
        out_specs=pl.BlockSpec(memory_space=pl.ANY),
        scratch_shapes=[
            pltpu.VMEM((2, tm * TOKEN_ROWS, LANES), F32),
            pltpu.VMEM((2, tm * TOKEN_ROWS, LANES), F32),
            pltpu.VMEM((tm, D_MODEL), F32),
            pltpu.VMEM((2, D_MODEL, D_EXPERT), BF16),
            pltpu.VMEM((2, D_MODEL, D_EXPERT), BF16),
            pltpu.VMEM((2, D_EXPERT, D_MODEL), BF16),
            pltpu.SemaphoreType.DMA((2,)),
            pltpu.SemaphoreType.DMA((2,)),
        ],
    )
    return pl.pallas_call(
        _moe_kernel,
        grid_spec=grid_spec,
        out_shape=jax.ShapeDtypeStruct((n_tok * TOKEN_ROWS, LANES), F32),
        compiler_params=_params(),
        name="moe",
    )(*scalars, x_tiles, w_pair, g, w_gate, w_up, w_down, w_gate, w_up, w_down)


def _final_kernel(x_ref, g_ref, o_ref):
    x = _load_token_tiles(x_ref, o_ref.shape[0])
    o_ref[...] = (x * _rms_scale(x)) * g_ref[...]


def _final_norm(x_tiles, g):
    t = x_tiles.shape[0] // TOKEN_ROWS
    tm = TM_DENSE
    return pl.pallas_call(
        _final_kernel,
        grid=(t // tm,),
        in_specs=[_x_spec(tm, True), pl.BlockSpec((1, D_MODEL), lambda i: (0, 0))],
        out_specs=_x_spec(tm, False),
        out_shape=jax.ShapeDtypeStruct((t, D_MODEL), F32),
        compiler_params=_params(),
        name="final_norm",
    )(x_tiles, g)


def _split_bf16(w):
    hi = w.astype(BF16)
    lo = (w - hi.astype(F32)).astype(BF16)
    return hi, lo


def kernel(x, norm_mix_g, w_in, rpb, pool_w, pool_scale, w_out, norm_ffn_g,
           w_router_group, w_router_expert, w_gate, w_up, w_down, final_g):
    batch, seq, d = x.shape
    depth = w_in.shape[0]
    assert d == D_MODEL and seq == GRID_H * GRID_W
    n_tok = batch * seq
    assert n_tok % TM_DENSE == 0 and n_tok % TM_MOE == 0

    bias = _attention_bias(rpb)
    cnt = jnp.asarray(_pool_counts(seq))
    n_pool = D_POOL // POOL_GROUP_DIM
    pool_scale4 = pool_scale.reshape(depth, n_pool, 1, POOL_GROUP_DIM)
    g_mix = norm_mix_g.reshape(depth, 1, D_MODEL)
    g_ffn = norm_ffn_g.reshape(depth, 1, D_MODEL)
    n_exp = N_GROUPS * EXPERTS_PER_GROUP
    w_gate_s = w_gate.reshape(depth * n_exp, D_MODEL, D_EXPERT)
    w_up_s = w_up.reshape(depth * n_exp, D_MODEL, D_EXPERT)
    w_down_s = w_down.reshape(depth * n_exp, D_EXPERT, D_MODEL)
    w_router = jnp.concatenate([w_router_group, w_router_expert], axis=-1).transpose(0, 2, 1)
    w_router = jnp.pad(w_router, ((0, 0), (0, ROUTER_ROWS - w_router.shape[1]), (0, 0)))
    w_router_b = jnp.concatenate(_split_bf16(w_router), axis=1)

    xs = x.reshape(n_tok, D_MODEL)
    for l in range(depth):
        tiled = l > 0
        proj = _in_proj(xs, tiled, g_mix, w_in, l, n_tok)
        a, p = _mixer(proj, bias, cnt, pool_w, pool_scale4, l, batch, seq)
        x_mid, routing = _out_proj(a, p, xs, tiled, w_out, g_ffn, w_router_b, l)
        xs = _moe(x_mid, routing, g_ffn, w_gate_s, w_up_s, w_down_s, l)
    out = _final_norm(xs, final_g.reshape(1, D_MODEL))
    return out.reshape(batch, seq, D_MODEL)
```

```python
import functools

import numpy as np
import jax
import jax.numpy as jnp
from jax import lax
from jax.experimental import pallas as pl
from jax.experimental.pallas import tpu as pltpu

F32 = jnp.float32
BF16 = jnp.bfloat16
I32 = jnp.int32

D_MODEL = 1024
GRID_W = 64
GRID_H = 32
D_ATTN = 512
HEAD_DIM = 64
N_HEADS = 8
WIN_H = 8
WIN_W = 16
D_POOL = 512
POOL_GROUP_DIM = 128
D_IN = 2048
N_GROUPS = 4
EXPERTS_PER_GROUP = 4
D_EXPERT = 512
N_CLASSES = 24
EPS = 1e-6
NEG_INF = -1e30
LOG2_E = 1.4426950408889634

LANES = 128
SUBLANES = 8
TOKEN_ROWS = D_MODEL // LANES
COL_BLOCK = 256
HEADS_PER_BLOCK = COL_BLOCK // HEAD_DIM
GROUPS_PER_BLOCK = COL_BLOCK // POOL_GROUP_DIM
N_COL_BLOCKS = D_ATTN // COL_BLOCK
Q_ROWS = 8
Q_COLS = 16
K_ROWS = 16
K_COLS = 32
Q_ROT = 8
ROW_BLOCK_KEY_START = (0, 4, 12, 16)
POOL_HALO = 16
POOL_WINDOWS = (2, 4, 8, 16)

TM_DENSE = 512
TM_MOE = 256
VMEM_LIMIT = 56 * 1024 * 1024

PAIR_LO = np.array([0, 0, 0, 1, 1, 2], np.int32)
PAIR_HI = np.array([1, 2, 3, 2, 3, 3], np.int32)


def _params(n_axes=1):
    return pltpu.CompilerParams(dimension_semantics=("arbitrary",) * n_axes, vmem_limit_bytes=VMEM_LIMIT)


def _rms_scale(x):
    return lax.rsqrt(jnp.mean(x * x, axis=-1, keepdims=True) + EPS)


def _load_token_tiles(ref, n_tok):
    return jnp.concatenate([ref[pl.ds(s, n_tok, stride=TOKEN_ROWS), :] for s in range(TOKEN_ROWS)], axis=1)


def _store_token_tiles(ref, val):
    n_tok = val.shape[0]
    for s in range(TOKEN_ROWS):
        ref[pl.ds(s, n_tok, stride=TOKEN_ROWS), :] = val[:, s * LANES:(s + 1) * LANES]


def _x_spec(tm, tiled):
    shape = (tm * TOKEN_ROWS, LANES) if tiled else (tm, D_MODEL)
    return pl.BlockSpec(shape, lambda i: (i, 0))


def _in_kernel(x_tiled, x_ref, g_ref, w_ref, proj_ref, w_bf):
    @pl.when(pl.program_id(0) == 0)
    def _():
        w_bf[...] = w_ref[0].astype(BF16)

    tm = proj_ref.shape[0]
    x = _load_token_tiles(x_ref, tm) if x_tiled else x_ref[...]
    h = ((x * _rms_scale(x)) * g_ref[0]).astype(BF16)
    proj = jnp.dot(h, w_bf[...], preferred_element_type=F32)
    q = (proj[:, :D_ATTN] * (HEAD_DIM ** -0.5 * LOG2_E)).reshape(tm // GRID_W, GRID_W, D_ATTN)
    q = jnp.concatenate([q[:, Q_ROT:], q[:, :Q_ROT]], axis=1).reshape(tm, D_ATTN)
    proj_ref[:, :D_ATTN] = q.astype(BF16)
    proj_ref[:, D_ATTN:] = proj[:, D_ATTN:].astype(BF16)


def _in_proj(x, x_tiled, g, w, layer, n_tok):
    tm = TM_DENSE
    fixed = lambda i: (layer, 0, 0)
    return pl.pallas_call(
        functools.partial(_in_kernel, x_tiled),
        grid=(n_tok // tm,),
        in_specs=[_x_spec(tm, x_tiled), pl.BlockSpec((1, 1, D_MODEL), fixed), pl.BlockSpec((1, D_MODEL, D_IN), fixed)],
        out_specs=pl.BlockSpec((tm, D_IN), lambda i: (i, 0)),
        out_shape=jax.ShapeDtypeStruct((n_tok, D_IN), BF16),
        scratch_shapes=[pltpu.VMEM((D_MODEL, D_IN), BF16)],
        compiler_params=_params(),
        name="in_proj",
    )(x, g, w)


def _key_col_segments(m):
    if m < 3:
        return ((Q_COLS * m, K_COLS),)
    return ((0, K_COLS // 2), (GRID_W - K_COLS // 2, K_COLS // 2))


def _key_col(m, kk):
    for start, n in _key_col_segments(m):
        if kk < n:
            return start + kk
        kk -= n
    raise ValueError(kk)


def _attention_index_tables():
    dr = np.zeros((3, Q_ROWS, K_ROWS), np.int32)
    rv = np.zeros((3, Q_ROWS, K_ROWS), bool)
    for typ, rb in enumerate((0, 1, 3)):
        for i in range(Q_ROWS):
            r = Q_ROWS * rb + i
            r_start = min(max(r - WIN_H // 2, 0), GRID_H - WIN_H)
            for kr in range(K_ROWS):
                ka = ROW_BLOCK_KEY_START[rb] + kr
                rv[typ, i, kr] = r_start <= ka < r_start + WIN_H
                dr[typ, i, kr] = min(max(ka - r + WIN_H - 1, 0), 2 * WIN_H - 2)
    dc = np.zeros((2, Q_COLS, K_COLS), np.int32)
    cv = np.zeros((2, Q_COLS, K_COLS), bool)
    for mt, m in enumerate((0, 3)):
        for j in range(Q_COLS):
            qc = (Q_COLS * m + j + Q_ROT) % GRID_W
            c_start = min(max(qc - WIN_W // 2, 0), GRID_W - WIN_W)
            for kk in range(K_COLS):
                kc = _key_col(m, kk)
                cv[mt, j, kk] = c_start <= kc < c_start + WIN_W
                dc[mt, j, kk] = min(max(kc - qc, -(WIN_W - 1)), WIN_W - 1) + WIN_W - 1
    return dr, rv, dc, cv


KEY_ROWS_PER_LANE_TILE = LANES // K_COLS


def _bias_kernel(row_index, e_ref, o_ref):
    for typ in range(row_index.shape[0]):
        for mt in range(2):
            for i in range(Q_ROWS):
                for c in range(K_ROWS // KEY_ROWS_PER_LANE_TILE):
                    tile = None
                    for s in range(KEY_ROWS_PER_LANE_TILE):
                        piece = e_ref[0, int(row_index[typ, i, KEY_ROWS_PER_LANE_TILE * c + s]), mt, s]
                        tile = piece if tile is None else tile + piece
                    o_ref[0, 2 * typ + mt, pl.ds(Q_COLS * i, Q_COLS), pl.ds(LANES * c, LANES)] = tile


def _attention_bias(rpb):
    dr, rv, dc, cv = _attention_index_tables()
    nl, nh, n_dr, _ = rpb.shape
    e = jnp.where(jnp.asarray(cv)[None, None, None], rpb[:, :, :, dc].astype(F32) * LOG2_E, NEG_INF)
    e = jnp.concatenate([e, jnp.full((nl, nh, 1) + e.shape[3:], NEG_INF, F32)], axis=2)
    slots = [jnp.pad(e, ((0, 0),) * 5 + ((K_COLS * s, LANES - K_COLS * (s + 1)),))
             for s in range(KEY_ROWS_PER_LANE_TILE)]
    e4 = jnp.stack(slots, axis=4).reshape(nl * nh, n_dr + 1, 2, KEY_ROWS_PER_LANE_TILE, Q_COLS, LANES)
    row_index = np.where(rv, dr, n_dr)
    nq, nk = Q_ROWS * Q_COLS, K_ROWS * K_COLS
    return pl.pallas_call(
        functools.partial(_bias_kernel, row_index),
        grid=(nl * nh,),
        in_specs=[pl.BlockSpec((1,) + e4.shape[1:], lambda i: (i, 0, 0, 0, 0, 0))],
        out_specs=pl.BlockSpec((1, 6, nq, nk), lambda i: (i, 0, 0, 0)),
        out_shape=jax.ShapeDtypeStruct((nl * nh, 6, nq, nk), F32),
        compiler_params=_params(),
        name="attention_bias",
    )(e4)


def _pool_counts(seq):
    t = np.arange(seq)
    cols = []
    for w in POOL_WINDOWS:
        cnt = np.minimum(t + w // 2, seq) - np.maximum(t - w // 2, 0)
        cols.append(np.broadcast_to(cnt[:, None].astype(np.float32), (seq, POOL_GROUP_DIM)))
    return np.concatenate(cols, axis=1).reshape(seq, N_COL_BLOCKS, COL_BLOCK).transpose(1, 0, 2)


def _window_sum(ue, window):
    n = ue.shape[0]

    def shifted(x, d):
        return pltpu.roll(x, d % n, axis=0)

    s = ue + shifted(ue, 1)
    reach = 1
    while 2 * reach < window:
        s = shifted(s, reach) + shifted(s, -reach)
        reach *= 2
    return s


def _mix_kernel(q_ref, k_ref, v_ref, u_ref, bias_ref, cnt_ref, pw_ref, ps_ref, a_ref, p_ref):
    jb = pl.program_id(0)
    seq = q_ref.shape[1]
    nq = Q_ROWS * Q_COLS
    nk = K_ROWS * K_COLS
    lane_head = lax.broadcasted_iota(I32, (nq, COL_BLOCK), 1) // HEAD_DIM

    def rows(ref, grid_row, col, n):
        return ref[0, pl.ds(pl.multiple_of(grid_row * GRID_W + col, Q_COLS), n), :]

    def row_block(rb, carry):
        key_row0 = jnp.clip(Q_ROWS * rb - WIN_H // 2, 0, GRID_H - K_ROWS)
        typ = jnp.where(rb == 0, 0, jnp.where(rb == GRID_H // Q_ROWS - 1, 2, 1))
        outs = []
        for m in range(GRID_W // Q_COLS):
            mt = 0 if m < 3 else 1
            qb = jnp.concatenate([rows(q_ref, Q_ROWS * rb + i, Q_COLS * m, Q_COLS) for i in range(Q_ROWS)], axis=0)
            segs = _key_col_segments(m)
            kb = jnp.concatenate(
                [rows(k_ref, key_row0 + kr, c0, n) for kr in range(K_ROWS) for c0, n in segs], axis=0)
            vb = jnp.concatenate(
                [rows(v_ref, key_row0 + kr, c0, n) for kr in range(K_ROWS) for c0, n in segs], axis=0)
            zero = jnp.zeros_like(qb)
            qh = jnp.concatenate([jnp.where(lane_head == h, qb, zero) for h in range(HEADS_PER_BLOCK)], axis=0)
            s = lax.dot_general(qh, kb, (((1,), (1,)), ((), ())), preferred_element_type=F32)
            s = s + jnp.concatenate([bias_ref[h, 2 * typ + mt] for h in range(HEADS_PER_BLOCK)], axis=0)
            mx = jnp.max(s, axis=-1, keepdims=True)
            e = jnp.exp2(s - mx)
            den = jnp.sum(e, axis=-1, keepdims=True)
            oh = jnp.dot(e.astype(BF16), vb, preferred_element_type=F32)
            o = oh[:nq]
            dn = jnp.broadcast_to(den[:nq], (nq, COL_BLOCK))
            for h in range(1, HEADS_PER_BLOCK):
                o = jnp.where(lane_head == h, oh[h * nq:(h + 1) * nq], o)
                dn = jnp.where(lane_head == h, den[h * nq:(h + 1) * nq], dn)
            outs.append(o / dn)
        for i in range(Q_ROWS):
            pieces = []
            for m in range(GRID_W // Q_COLS):
                pieces.append(outs[(m - 1) % 4][Q_COLS * i + Q_ROT:Q_COLS * (i + 1)])
                pieces.append(outs[m][Q_COLS * i:Q_COLS * i + Q_ROT])
            start = pl.multiple_of((Q_ROWS * rb + i) * GRID_W, GRID_W)
            a_ref[0, pl.ds(start, GRID_W), :] = jnp.concatenate(pieces, axis=0).astype(BF16)
        return carry

    lax.fori_loop(0, GRID_H // Q_ROWS, row_block, 0)

    def pool(windows):
        u = u_ref[0].astype(F32)
        pad = jnp.zeros((POOL_HALO, POOL_GROUP_DIM), F32)
        for gi, window in enumerate(windows):
            sl = slice(gi * POOL_GROUP_DIM, (gi + 1) * POOL_GROUP_DIM)
            ug = u[:, sl]
            win = _window_sum(jnp.concatenate([pad, ug, pad], axis=0), window)[POOL_HALO:POOL_HALO + seq]
            pooled = win / cnt_ref[0, :, sl] - ug
            mixed = jnp.dot(pooled.astype(BF16), pw_ref[0, gi].astype(BF16), preferred_element_type=F32) * ps_ref[0, gi]
            p_ref[0, :, sl] = mixed.astype(BF16)

    for cb in range(N_COL_BLOCKS):
        @pl.when(jb == cb)
        def _(cb=cb):
            pool(POOL_WINDOWS[cb * GROUPS_PER_BLOCK:(cb + 1) * GROUPS_PER_BLOCK])


def _mixer(proj, bias, cnt, pool_w, pool_scale, layer, batch, seq):
    proj3 = proj.reshape(batch, seq, D_IN)
    col = lambda base: (lambda j, b: (b, 0, base + j))
    blk = (1, seq, COL_BLOCK)
    bias_blk = (HEADS_PER_BLOCK,) + bias.shape[1:]
    a, p = pl.pallas_call(
        _mix_kernel,
        grid=(N_COL_BLOCKS, batch),
        in_specs=[
            pl.BlockSpec(blk, col(0)),
            pl.BlockSpec(blk, col(N_COL_BLOCKS)),
            pl.BlockSpec(blk, col(2 * N_COL_BLOCKS)),
            pl.BlockSpec(blk, col(3 * N_COL_BLOCKS)),
            pl.BlockSpec(bias_blk, lambda j, b: (layer * N_COL_BLOCKS + j, 0, 0, 0)),
            pl.BlockSpec(blk, lambda j, b: (j, 0, 0)),
            pl.BlockSpec((1, GROUPS_PER_BLOCK, POOL_GROUP_DIM, POOL_GROUP_DIM), lambda j, b: (layer, j, 0, 0)),
            pl.BlockSpec((1, GROUPS_PER_BLOCK, 1, POOL_GROUP_DIM), lambda j, b: (layer, j, 0, 0)),
        ],
        out_specs=[pl.BlockSpec(blk, lambda j, b: (b, 0, j)), pl.BlockSpec(blk, lambda j, b: (b, 0, j))],
        out_shape=[jax.ShapeDtypeStruct((batch, seq, D_ATTN), BF16), jax.ShapeDtypeStruct((batch, seq, D_POOL), BF16)],
        compiler_params=_params(2),
        name="mixer",
    )(proj3, proj3, proj3, proj3, bias, cnt, pool_w, pool_scale)
    return a.reshape(batch * seq, D_ATTN), p.reshape(batch * seq, D_POOL)


ROUTER_ROWS = 32
OUT_SPLIT = 2


def _first_max(vals):
    best = vals[0]
    for v in vals[1:]:
        best = jnp.maximum(best, v)
    idx = jnp.float32(len(vals) - 1)
    for i in range(len(vals) - 2, -1, -1):
        idx = jnp.where(vals[i] == best, jnp.float32(i), idx)
    return best, idx


def _route(lg, norm_scale):
    n = lg.shape[1]
    big = jnp.float32(-3e38)
    g = [lg[i:i + 1] for i in range(N_GROUPS)]
    gmax, gidx = _first_max(g)
    gsum = jnp.exp(g[0] - gmax)
    for v in g[1:]:
        gsum = gsum + jnp.exp(v - gmax)
    gw = 1.0 / gsum
    e = []
    for j in range(EXPERTS_PER_GROUP):
        v = lg[N_GROUPS + j:N_GROUPS + j + 1]
        for gg in range(1, N_GROUPS):
            row = N_GROUPS + EXPERTS_PER_GROUP * gg + j
            v = jnp.where(gidx == gg, lg[row:row + 1], v)
        e.append(v)
    v1, i1 = _first_max(e)
    v2, i2 = _first_max([jnp.where(i1 == j, big, e[j]) for j in range(EXPERTS_PER_GROUP)])
    d = jnp.exp(v2 - v1)
    w1 = gw * (1.0 / (1.0 + d))
    w2 = gw * (d / (1.0 + d))
    first_lo = i1 < i2
    lo = jnp.where(first_lo, i1, i2)
    hi = jnp.where(first_lo, i2, i1)
    w_lo = jnp.where(first_lo, w1, w2)
    w_hi = jnp.where(first_lo, w2, w1)
    cls = 6.0 * gidx + 0.5 * (lo * (7.0 - lo)) + (hi - lo - 1.0)
    return jnp.concatenate([w_lo, w_hi, cls, norm_scale, jnp.zeros((SUBLANES - 4, n), F32)], axis=0)


def _out_kernel(x_tiled, a_ref, p_ref, x_ref, wa_ref, wp_ref, g_ref, wr_ref, x2_ref, r_ref, wa_bf, wp_bf):
    @pl.when(pl.program_id(0) == 0)
    def _():
        wa_bf[...] = wa_ref[0].astype(BF16)
        wp_bf[...] = wp_ref[0].astype(BF16)

    tm = a_ref.shape[0]
    n = tm // OUT_SPLIT
    nt_dims = (((1,), (1,)), ((), ()))
    for c in range(OUT_SPLIT):
        sl = pl.ds(c * n, n)
        tiles = pl.ds(c * n * TOKEN_ROWS, n * TOKEN_ROWS)
        mix = jnp.dot(a_ref[sl, :], wa_bf[...], preferred_element_type=F32)
        mix = mix + jnp.dot(p_ref[sl, :], wp_bf[...], preferred_element_type=F32)
        x2 = (_load_token_tiles(x_ref.at[tiles], n) if x_tiled else x_ref[sl, :]) + mix
        _store_token_tiles(x2_ref.at[tiles], x2)
        scale = _rms_scale(x2)
        scale_row = jnp.transpose(jnp.broadcast_to(scale, (n, LANES)))[:1]
        h = (x2 * scale) * g_ref[0]
        hh = h.astype(BF16)
        hl = (h - hh.astype(F32)).astype(BF16)
        r1 = lax.dot_general(wr_ref[0], hh, nt_dims, preferred_element_type=F32)
        r2 = lax.dot_general(wr_ref[0, :ROUTER_ROWS], hl, nt_dims, preferred_element_type=F32)
        r_ref[0, :, sl] = _route(r1[:ROUTER_ROWS] + r1[ROUTER_ROWS:] + r2, scale_row)


def _out_proj(a, p, x, x_tiled, w_out, g, wr, layer):
    t = a.shape[0]
    tm = TM_DENSE
    row = lambda i: (i, 0)
    fixed = lambda i: (layer, 0, 0)
    return pl.pallas_call(
        functools.partial(_out_kernel, x_tiled),
        grid=(t // tm,),
        in_specs=[
            pl.BlockSpec((tm, D_ATTN), row),
            pl.BlockSpec((tm, D_POOL), row),
            _x_spec(tm, x_tiled),
            pl.BlockSpec((1, D_ATTN, D_MODEL), fixed),
            pl.BlockSpec((1, D_POOL, D_MODEL), lambda i: (layer, 1, 0)),
            pl.BlockSpec((1, 1, D_MODEL), fixed),
            pl.BlockSpec((1, 2 * ROUTER_ROWS, D_MODEL), fixed),
        ],
        out_specs=[_x_spec(tm, True),
                   pl.BlockSpec((1, SUBLANES, tm), lambda i: (i, 0, 0))],
        out_shape=[jax.ShapeDtypeStruct((t * TOKEN_ROWS, LANES), F32),
                   jax.ShapeDtypeStruct((t // tm, SUBLANES, tm), F32)],
        scratch_shapes=[pltpu.VMEM((D_ATTN, D_MODEL), BF16), pltpu.VMEM((D_POOL, D_MODEL), BF16)],
        compiler_params=_params(),
        name="out_proj",
    )(a, p, x, w_out, w_out, g, wr)


def _moe_kernel(tok_ref, tile_ref, first_ref, last_ref, lo_ref, hi_ref, newc_ref, elo_ref, ehi_ref,
                h_hbm, w_ref, g_ref, wg_lo, wu_lo, wd_lo, wg_hi, wu_hi, wd_hi, y_hbm,
                xbuf, ybuf, acc, wg_bf, wu_bf, wd_bf, gsem, ssem):
    del elo_ref, ehi_ref
    k = pl.program_id(0)
    tm = acc.shape[0]
    n_tiles = y_hbm.shape[0] // (tm * TOKEN_ROWS)
    tile = tile_ref[k]
    slot = tile % 2

    @pl.when(newc_ref[k] == 1)
    def _():
        for e, (wg, wu, wd) in enumerate(((wg_lo, wu_lo, wd_lo), (wg_hi, wu_hi, wd_hi))):
            wg_bf[e] = wg[0].astype(BF16)
            wu_bf[e] = wu[0].astype(BF16)
            wd_bf[e] = wd[0].astype(BF16)

    def token_rows(idx):
        return pl.ds(pl.multiple_of(idx * TOKEN_ROWS, TOKEN_ROWS), TOKEN_ROWS)

    def gather_copy(j, tok, sl):
        return pltpu.make_async_copy(h_hbm.at[token_rows(tok)], xbuf.at[sl, token_rows(j)], gsem.at[sl])

    def scatter_copy(j, tok, sl):
        return pltpu.make_async_copy(ybuf.at[sl, token_rows(j)], y_hbm.at[token_rows(tok)], ssem.at[sl])

    def start_rows(copy, tl, sl):
        def body(j, c):
            copy(j, tok_ref[tl * tm + j], sl).start()
            return c
        lax.fori_loop(0, tm, body, 0, unroll=16)

    def wait_rows(copy, sl):
        for j in range(tm):
            copy(j, 0, sl).wait()

    @pl.when(first_ref[k] == 1)
    def _():
        @pl.when(k == 0)
        def _():
            start_rows(gather_copy, tile, slot)
        wait_rows(gather_copy, slot)

        @pl.when(tile + 1 < n_tiles)
        def _():
            start_rows(gather_copy, tile + 1, 1 - slot)

        @pl.when(tile >= 2)
        def _():
            wait_rows(scatter_copy, slot)

    lo = lo_ref[k]
    hi = hi_ref[k]

    @pl.when(hi > lo)
    def _():
        w = w_ref[...]
        x = ((_load_token_tiles(xbuf.at[slot], tm) * w[:, 2:3]) * g_ref[0]).astype(BF16)
        row = lax.broadcasted_iota(I32, (tm, 1), 0)
        mine = (row >= lo) & (row < hi)
        w_lo = jnp.where(mine, w[:, 0:1], 0.0)
        w_hi = jnp.where(mine, w[:, 1:2], 0.0)

        def expert(e):
            a = jax.nn.silu(jnp.dot(x, wg_bf[e], preferred_element_type=F32)) * jnp.dot(x, wu_bf[e], preferred_element_type=F32)
            return jnp.dot(a.astype(BF16), wd_bf[e], preferred_element_type=F32)

        y = w_lo * expert(0) + w_hi * expert(1)

        @pl.when(lo == 0)
        def _():
            acc[...] = y

        @pl.when(lo > 0)
        def _():
            acc[...] += y

    @pl.when(last_ref[k] == 1)
    def _():
        xs = xbuf.at[slot]
        ys = ybuf.at[slot]
        for s in range(TOKEN_ROWS):
            rows_s = pl.ds(s, tm, stride=TOKEN_ROWS)
            ys[rows_s, :] = xs[rows_s, :] + acc[:, s * LANES:(s + 1) * LANES]
        start_rows(scatter_copy, tile, slot)

    @pl.when(k == pl.num_programs(0) - 1)
    def _():
        wait_rows(scatter_copy, slot)
        if n_tiles > 1:
            wait_rows(scatter_copy, 1 - slot)


def _moe_tables(routing, n_tok, tm, expert0):
    w_lo = routing[:, 0, :].reshape(n_tok)
    w_hi = routing[:, 1, :].reshape(n_tok)
    cls = routing[:, 2, :].reshape(n_tok).astype(I32)
    norm_scale = routing[:, 3, :].reshape(n_tok)
    key = cls * n_tok + jnp.arange(n_tok, dtype=I32)
    skey, w_lo, w_hi, norm_scale = lax.sort((key, w_lo, w_hi, norm_scale), num_keys=1)
    tok_sorted = skey % n_tok
    bounds = jnp.arange(N_CLASSES + 1, dtype=I32) * n_tok
    cstart = jnp.sum((skey[None, :] < bounds[:, None]).astype(I32), axis=1)
    n_tiles = n_tok // tm
    cuts = jnp.sort(jnp.concatenate([jnp.arange(n_tiles, dtype=I32) * tm, cstart[1:N_CLASSES]]))
    ends = jnp.concatenate([cuts[1:], jnp.full((1,), n_tok, I32)])
    tile = jnp.minimum(cuts // tm, n_tiles - 1)
    c = jnp.clip(jnp.sum((cstart[None, :] <= cuts[:, None]).astype(I32), axis=1) - 1, 0, N_CLASSES - 1)
    lo = cuts - tile * tm
    hi = ends - tile * tm
    change = (tile[1:] != tile[:-1]).astype(I32)
    one = jnp.ones((1,), I32)
    first = jnp.concatenate([one, change])
    last = jnp.concatenate([change, one])
    new_class = jnp.concatenate([one, (c[1:] != c[:-1]).astype(I32)])
    grp = c // 6
    e_lo = expert0 + EXPERTS_PER_GROUP * grp + jnp.asarray(PAIR_LO)[c % 6]
    e_hi = expert0 + EXPERTS_PER_GROUP * grp + jnp.asarray(PAIR_HI)[c % 6]
    scalars = tuple(v.astype(I32) for v in (tok_sorted, tile, first, last, lo, hi, new_class, e_lo, e_hi))
    return scalars, jnp.stack([w_lo, w_hi, norm_scale], axis=1)


def _moe(x_tiles, routing, g, w_gate, w_up, w_down, layer):
    n_tok = x_tiles.shape[0] // TOKEN_ROWS
    tm = TM_MOE
    scalars, w_pair = _moe_tables(routing, n_tok, tm, layer * N_GROUPS * EXPERTS_PER_GROUP)
    n_steps = scalars[1].shape[0]
    by_tile = lambda k, tok, tile, first, last, lo, hi, newc, elo, ehi: (tile[k], 0)
    e_lo = lambda k, tok, tile, first, last, lo, hi, newc, elo, ehi: (elo[k], 0, 0)
    e_hi = lambda k, tok, tile, first, last, lo, hi, newc, elo, ehi: (ehi[k], 0, 0)
    up_blk = (1, D_MODEL, D_EXPERT)
    down_blk = (1, D_EXPERT, D_MODEL)
    grid_spec = pltpu.PrefetchScalarGridSpec(
        num_scalar_prefetch=len(scalars),
        grid=(n_steps,),
        in_specs=[
            pl.BlockSpec(memory_space=pl.ANY),
            pl.BlockSpec((tm, 3), by_tile),
            pl.BlockSpec((1, 1, D_MODEL), lambda k, *_: (layer, 0, 0)),
            pl.BlockSpec(up_blk, e_lo), pl.BlockSpec(up_blk, e_lo), pl.BlockSpec(down_blk, e_lo),
            pl.BlockSpec(up_blk, e_hi), pl.BlockSpec(up_blk, e_hi), pl.BlockSpec(down_blk, e_hi),
        ],
        out_specs=pl.BlockSpec(memory_space=pl.ANY),
        scratch_shapes=[
            pltpu.VMEM((2, tm * TOKEN_ROWS, LANES), F32),
            pltpu.VMEM((2, tm * TOKEN_ROWS, LANES), F32),
            pltpu.VMEM((tm, D_MODEL), F32),
            pltpu.VMEM((2, D_MODEL, D_EXPERT), BF16),
            pltpu.VMEM((2, D_MODEL, D_EXPERT), BF16),
            pltpu.VMEM((2, D_EXPERT, D_MODEL), BF16),
            pltpu.SemaphoreType.DMA((2,)),
            pltpu.SemaphoreType.DMA((2,)),
        ],
    )
    return pl.pallas_call(
        _moe_kernel,
        grid_spec=grid_spec,
        out_shape=jax.ShapeDtypeStruct((n_tok * TOKEN_ROWS, LANES), F32),
        compiler_params=_params(),
        name="moe",
    )(*scalars, x_tiles, w_pair, g, w_gate, w_up, w_down, w_gate, w_up, w_down)


def _final_kernel(x_ref, g_ref, o_ref):
    x = _load_token_tiles(x_ref, o_ref.shape[0])
    o_ref[...] = (x * _rms_scale(x)) * g_ref[...]


def _final_norm(x_tiles, g):
    t = x_tiles.shape[0] // TOKEN_ROWS
    tm = TM_DENSE
    return pl.pallas_call(
        _final_kernel,
        grid=(t // tm,),
        in_specs=[_x_spec(tm, True), pl.BlockSpec((1, D_MODEL), lambda i: (0, 0))],
        out_specs=_x_spec(tm, False),
        out_shape=jax.ShapeDtypeStruct((t, D_MODEL), F32),
        compiler_params=_params(),
        name="final_norm",
    )(x_tiles, g)


def _split_bf16(w):
    hi = w.astype(BF16)
    lo = (w - hi.astype(F32)).astype(BF16)
    return hi, lo


def kernel(x, norm_mix_g, w_in, rpb, pool_w, pool_scale, w_out, norm_ffn_g,
           w_router_group, w_router_expert, w_gate, w_up, w_down, final_g):
    batch, seq, d = x.shape
    depth = w_in.shape[0]
    assert d == D_MODEL and seq == GRID_H * GRID_W
    n_tok = batch * seq
    assert n_tok % TM_DENSE == 0 and n_tok % TM_MOE == 0

    bias = _attention_bias(rpb)
    cnt = jnp.asarray(_pool_counts(seq))
    n_pool = D_POOL // POOL_GROUP_DIM
    pool_scale4 = pool_scale.reshape(depth, n_pool, 1, POOL_GROUP_DIM)
    g_mix = norm_mix_g.reshape(depth, 1, D_MODEL)
    g_ffn = norm_ffn_g.reshape(depth, 1, D_MODEL)
    n_exp = N_GROUPS * EXPERTS_PER_GROUP
    w_gate_s = w_gate.reshape(depth * n_exp, D_MODEL, D_EXPERT)
    w_up_s = w_up.reshape(depth * n_exp, D_MODEL, D_EXPERT)
    w_down_s = w_down.reshape(depth * n_exp, D_EXPERT, D_MODEL)
    w_router = jnp.concatenate([w_router_group, w_router_expert], axis=-1).transpose(0, 2, 1)
    w_router = jnp.pad(w_router, ((0, 0), (0, ROUTER_ROWS - w_router.shape[1]), (0, 0)))
    w_router_b = jnp.concatenate(_split_bf16(w_router), axis=1)

    xs = x.reshape(n_tok, D_MODEL)
    for l in range(depth):
        tiled = l > 0
        proj = _in_proj(xs, tiled, g_mix, w_in, l, n_tok)
        a, p = _mixer(proj, bias, cnt, pool_w, pool_scale4, l, batch, seq)
        x_mid, routing = _out_proj(a, p, xs, tiled, w_out, g_ffn, w_router_b, l)
        xs = _moe(x_mid, routing, g_ffn, w_gate_s, w_up_s, w_down_s, l)
    out = _final_norm(xs, final_g.reshape(1, D_MODEL))
    return out.reshape(batch, seq, D_MODEL)
```

```python
import functools

import numpy as np
import jax
import jax.numpy as jnp
from jax import lax
from jax.experimental import pallas as pl
from jax.experimental.pallas import tpu as pltpu

F32 = jnp.float32
BF16 = jnp.bfloat16
I32 = jnp.int32

D_MODEL = 1024
GRID_W = 64
GRID_H = 32
D_ATTN = 512
HEAD_DIM = 64
N_HEADS = 8
WIN_H = 8
WIN_W = 16
D_POOL = 512
POOL_GROUP_DIM = 128
D_IN = 2048
N_GROUPS = 4
EXPERTS_PER_GROUP = 4
D_EXPERT = 512
N_CLASSES = 24
EPS = 1e-6
NEG_INF = -1e30
LOG2_E = 1.4426950408889634

LANES = 128
SUBLANES = 8
TOKEN_ROWS = D_MODEL // LANES
COL_BLOCK = 256
HEADS_PER_BLOCK = COL_BLOCK // HEAD_DIM
GROUPS_PER_BLOCK = COL_BLOCK // POOL_GROUP_DIM
N_COL_BLOCKS = D_ATTN // COL_BLOCK
Q_ROWS = 8
Q_COLS = 16
K_ROWS = 16
K_COLS = 32
Q_ROT = 8
ROW_BLOCK_KEY_START = (0, 4, 12, 16)
POOL_HALO = 16
POOL_WINDOWS = (2, 4, 8, 16)

TM_DENSE = 512
TM_OUT = 1024
TM_MOE = 256
VMEM_LIMIT = 56 * 1024 * 1024

PAIR_LO = np.array([0, 0, 0, 1, 1, 2], np.int32)
PAIR_HI = np.array([1, 2, 3, 2, 3, 3], np.int32)


def _params(n_axes=1):
    return pltpu.CompilerParams(dimension_semantics=("arbitrary",) * n_axes, vmem_limit_bytes=VMEM_LIMIT)


def _rms_scale(x):
    return lax.rsqrt(jnp.mean(x * x, axis=-1, keepdims=True) + EPS)


def _load_token_tiles(ref, n_tok):
    return jnp.concatenate([ref[pl.ds(s, n_tok, stride=TOKEN_ROWS), :] for s in range(TOKEN_ROWS)], axis=1)


def _store_token_tiles(ref, val):
    n_tok = val.shape[0]
    for s in range(TOKEN_ROWS):
        ref[pl.ds(s, n_tok, stride=TOKEN_ROWS), :] = val[:, s * LANES:(s + 1) * LANES]


def _x_spec(tm, tiled):
    shape = (tm * TOKEN_ROWS, LANES) if tiled else (tm, D_MODEL)
    return pl.BlockSpec(shape, lambda i: (i, 0))


def _in_kernel(x_tiled, x_ref, g_ref, w_ref, proj_ref, w_bf):
    @pl.when(pl.program_id(0) == 0)
    def _():
        w_bf[...] = w_ref[0].astype(BF16)

    tm = proj_ref.shape[0]
    x = _load_token_tiles(x_ref, tm) if x_tiled else x_ref[...]
    h = ((x * _rms_scale(x)) * g_ref[0]).astype(BF16)
    proj = jnp.dot(h, w_bf[...], preferred_element_type=F32)
    q = (proj[:, :D_ATTN] * (HEAD_DIM ** -0.5 * LOG2_E)).reshape(tm // GRID_W, GRID_W, D_ATTN)
    q = jnp.concatenate([q[:, Q_ROT:], q[:, :Q_ROT]], axis=1).reshape(tm, D_ATTN)
    proj_ref[:, :D_ATTN] = q.astype(BF16)
    proj_ref[:, D_ATTN:] = proj[:, D_ATTN:].astype(BF16)


def _in_proj(x, x_tiled, g, w, layer, n_tok):
    tm = TM_DENSE
    fixed = lambda i: (layer, 0, 0)
    return pl.pallas_call(
        functools.partial(_in_kernel, x_tiled),
        grid=(n_tok // tm,),
        in_specs=[_x_spec(tm, x_tiled), pl.BlockSpec((1, 1, D_MODEL), fixed), pl.BlockSpec((1, D_MODEL, D_IN), fixed)],
        out_specs=pl.BlockSpec((tm, D_IN), lambda i: (i, 0)),
        out_shape=jax.ShapeDtypeStruct((n_tok, D_IN), BF16),
        scratch_shapes=[pltpu.VMEM((D_MODEL, D_IN), BF16)],
        compiler_params=_params(),
        name="in_proj",
    )(x, g, w)


def _key_col_segments(m):
    if m < 3:
        return ((Q_COLS * m, K_COLS),)
    return ((0, K_COLS // 2), (GRID_W - K_COLS // 2, K_COLS // 2))


def _key_col(m, kk):
    for start, n in _key_col_segments(m):
        if kk < n:
            return start + kk
        kk -= n
    raise ValueError(kk)


def _attention_index_tables():
    dr = np.zeros((3, Q_ROWS, K_ROWS), np.int32)
    rv = np.zeros((3, Q_ROWS, K_ROWS), bool)
    for typ, rb in enumerate((0, 1, 3)):
        for i in range(Q_ROWS):
            r = Q_ROWS * rb + i
            r_start = min(max(r - WIN_H // 2, 0), GRID_H - WIN_H)
            for kr in range(K_ROWS):
                ka = ROW_BLOCK_KEY_START[rb] + kr
                rv[typ, i, kr] = r_start <= ka < r_start + WIN_H
                dr[typ, i, kr] = min(max(ka - r + WIN_H - 1, 0), 2 * WIN_H - 2)
    dc = np.zeros((2, Q_COLS, K_COLS), np.int32)
    cv = np.zeros((2, Q_COLS, K_COLS), bool)
    for mt, m in enumerate((0, 3)):
        for j in range(Q_COLS):
            qc = (Q_COLS * m + j + Q_ROT) % GRID_W
            c_start = min(max(qc - WIN_W // 2, 0), GRID_W - WIN_W)
            for kk in range(K_COLS):
                kc = _key_col(m, kk)
                cv[mt, j, kk] = c_start <= kc < c_start + WIN_W
                dc[mt, j, kk] = min(max(kc - qc, -(WIN_W - 1)), WIN_W - 1) + WIN_W - 1
    return dr, rv, dc, cv


KEY_ROWS_PER_LANE_TILE = LANES // K_COLS


def _bias_kernel(row_index, e_ref, o_ref):
    for typ in range(row_index.shape[0]):
        for mt in range(2):
            for i in range(Q_ROWS):
                for c in range(K_ROWS // KEY_ROWS_PER_LANE_TILE):
                    tile = None
                    for s in range(KEY_ROWS_PER_LANE_TILE):
                        piece = e_ref[0, int(row_index[typ, i, KEY_ROWS_PER_LANE_TILE * c + s]), mt, s]
                        tile = piece if tile is None else tile + piece
                    o_ref[0, 2 * typ + mt, pl.ds(Q_COLS * i, Q_COLS), pl.ds(LANES * c, LANES)] = tile


def _attention_bias(rpb):
    dr, rv, dc, cv = _attention_index_tables()
    nl, nh, n_dr, _ = rpb.shape
    e = jnp.where(jnp.asarray(cv)[None, None, None], rpb[:, :, :, dc].astype(F32) * LOG2_E, NEG_INF)
    e = jnp.concatenate([e, jnp.full((nl, nh, 1) + e.shape[3:], NEG_INF, F32)], axis=2)
    slots = [jnp.pad(e, ((0, 0),) * 5 + ((K_COLS * s, LANES - K_COLS * (s + 1)),))
             for s in range(KEY_ROWS_PER_LANE_TILE)]
    e4 = jnp.stack(slots, axis=4).reshape(nl * nh, n_dr + 1, 2, KEY_ROWS_PER_LANE_TILE, Q_COLS, LANES)
    row_index = np.where(rv, dr, n_dr)
    nq, nk = Q_ROWS * Q_COLS, K_ROWS * K_COLS
    return pl.pallas_call(
        functools.partial(_bias_kernel, row_index),
        grid=(nl * nh,),
        in_specs=[pl.BlockSpec((1,) + e4.shape[1:], lambda i: (i, 0, 0, 0, 0, 0))],
        out_specs=pl.BlockSpec((1, 6, nq, nk), lambda i: (i, 0, 0, 0)),
        out_shape=jax.ShapeDtypeStruct((nl * nh, 6, nq, nk), F32),
        compiler_params=_params(),
        name="attention_bias",
    )(e4)


def _pool_counts(seq):
    t = np.arange(seq)
    cols = []
    for w in POOL_WINDOWS:
        cnt = np.minimum(t + w // 2, seq) - np.maximum(t - w // 2, 0)
        cols.append(np.broadcast_to(cnt[:, None].astype(np.float32), (seq, POOL_GROUP_DIM)))
    return np.concatenate(cols, axis=1).reshape(seq, N_COL_BLOCKS, COL_BLOCK).transpose(1, 0, 2)


def _window_sum(ue, window):
    n = ue.shape[0]

    def shifted(x, d):
        return pltpu.roll(x, d % n, axis=0)

    s = ue + shifted(ue, 1)
    reach = 1
    while 2 * reach < window:
        s = shifted(s, reach) + shifted(s, -reach)
        reach *= 2
    return s


def _mix_kernel(q_ref, k_ref, v_ref, u_ref, bias_ref, cnt_ref, pw_ref, ps_ref, a_ref, p_ref):
    jb = pl.program_id(0)
    seq = q_ref.shape[1]
    nq = Q_ROWS * Q_COLS
    nk = K_ROWS * K_COLS
    lane_head = lax.broadcasted_iota(I32, (nq, COL_BLOCK), 1) // HEAD_DIM

    def rows(ref, grid_row, col, n):
        return ref[0, pl.ds(pl.multiple_of(grid_row * GRID_W + col, Q_COLS), n), :]

    def row_block(rb, carry):
        key_row0 = jnp.clip(Q_ROWS * rb - WIN_H // 2, 0, GRID_H - K_ROWS)
        typ = jnp.where(rb == 0, 0, jnp.where(rb == GRID_H // Q_ROWS - 1, 2, 1))
        outs = []
        for m in range(GRID_W // Q_COLS):
            mt = 0 if m < 3 else 1
            qb = jnp.concatenate([rows(q_ref, Q_ROWS * rb + i, Q_COLS * m, Q_COLS) for i in range(Q_ROWS)], axis=0)
            segs = _key_col_segments(m)
            kb = jnp.concatenate(
                [rows(k_ref, key_row0 + kr, c0, n) for kr in range(K_ROWS) for c0, n in segs], axis=0)
            vb = jnp.concatenate(
                [rows(v_ref, key_row0 + kr, c0, n) for kr in range(K_ROWS) for c0, n in segs], axis=0)
            zero = jnp.zeros_like(qb)
            qh = jnp.concatenate([jnp.where(lane_head == h, qb, zero) for h in range(HEADS_PER_BLOCK)], axis=0)
            s = lax.dot_general(qh, kb, (((1,), (1,)), ((), ())), preferred_element_type=F32)
            s = s + jnp.concatenate([bias_ref[h, 2 * typ + mt] for h in range(HEADS_PER_BLOCK)], axis=0)
            mx = jnp.max(s, axis=-1, keepdims=True)
            e = jnp.exp2(s - mx)
            den = jnp.sum(e, axis=-1, keepdims=True)
            oh = jnp.dot(e.astype(BF16), vb, preferred_element_type=F32)
            o = oh[:nq]
            dn = jnp.broadcast_to(den[:nq], (nq, COL_BLOCK))
            for h in range(1, HEADS_PER_BLOCK):
                o = jnp.where(lane_head == h, oh[h * nq:(h + 1) * nq], o)
                dn = jnp.where(lane_head == h, den[h * nq:(h + 1) * nq], dn)
            outs.append(o / dn)
        for i in range(Q_ROWS):
            pieces = []
            for m in range(GRID_W // Q_COLS):
                pieces.append(outs[(m - 1) % 4][Q_COLS * i + Q_ROT:Q_COLS * (i + 1)])
                pieces.append(outs[m][Q_COLS * i:Q_COLS * i + Q_ROT])
            start = pl.multiple_of((Q_ROWS * rb + i) * GRID_W, GRID_W)
            a_ref[0, pl.ds(start, GRID_W), :] = jnp.concatenate(pieces, axis=0).astype(BF16)
        return carry

    lax.fori_loop(0, GRID_H // Q_ROWS, row_block, 0)

    def pool(windows):
        u = u_ref[0].astype(F32)
        pad = jnp.zeros((POOL_HALO, POOL_GROUP_DIM), F32)
        for gi, window in enumerate(windows):
            sl = slice(gi * POOL_GROUP_DIM, (gi + 1) * POOL_GROUP_DIM)
            ug = u[:, sl]
            win = _window_sum(jnp.concatenate([pad, ug, pad], axis=0), window)[POOL_HALO:POOL_HALO + seq]
            pooled = win / cnt_ref[0, :, sl] - ug
            mixed = jnp.dot(pooled.astype(BF16), pw_ref[0, gi].astype(BF16), preferred_element_type=F32) * ps_ref[0, gi]
            p_ref[0, :, sl] = mixed.astype(BF16)

    for cb in range(N_COL_BLOCKS):
        @pl.when(jb == cb)
        def _(cb=cb):
            pool(POOL_WINDOWS[cb * GROUPS_PER_BLOCK:(cb + 1) * GROUPS_PER_BLOCK])


def _mixer(proj, bias, cnt, pool_w, pool_scale, layer, batch, seq):
    proj3 = proj.reshape(batch, seq, D_IN)
    col = lambda base: (lambda j, b: (b, 0, base + j))
    blk = (1, seq, COL_BLOCK)
    bias_blk = (HEADS_PER_BLOCK,) + bias.shape[1:]
    a, p = pl.pallas_call(
        _mix_kernel,
        grid=(N_COL_BLOCKS, batch),
        in_specs=[
            pl.BlockSpec(blk, col(0)),
            pl.BlockSpec(blk, col(N_COL_BLOCKS)),
            pl.BlockSpec(blk, col(2 * N_COL_BLOCKS)),
            pl.BlockSpec(blk, col(3 * N_COL_BLOCKS)),
            pl.BlockSpec(bias_blk, lambda j, b: (layer * N_COL_BLOCKS + j, 0, 0, 0)),
            pl.BlockSpec(blk, lambda j, b: (j, 0, 0)),
            pl.BlockSpec((1, GROUPS_PER_BLOCK, POOL_GROUP_DIM, POOL_GROUP_DIM), lambda j, b: (layer, j, 0, 0)),
            pl.BlockSpec((1, GROUPS_PER_BLOCK, 1, POOL_GROUP_DIM), lambda j, b: (layer, j, 0, 0)),
        ],
        out_specs=[pl.BlockSpec(blk, lambda j, b: (b, 0, j)), pl.BlockSpec(blk, lambda j, b: (b, 0, j))],
        out_shape=[jax.ShapeDtypeStruct((batch, seq, D_ATTN), BF16), jax.ShapeDtypeStruct((batch, seq, D_POOL), BF16)],
        compiler_params=_params(2),
        name="mixer",
    )(proj3, proj3, proj3, proj3, bias, cnt, pool_w, pool_scale)
    return a.reshape(batch * seq, D_ATTN), p.reshape(batch * seq, D_POOL)


ROUTER_ROWS = 32
OUT_SPLIT = 4


def _first_max(vals):
    best = vals[0]
    for v in vals[1:]:
        best = jnp.maximum(best, v)
    idx = jnp.float32(len(vals) - 1)
    for i in range(len(vals) - 2, -1, -1):
        idx = jnp.where(vals[i] == best, jnp.float32(i), idx)
    return best, idx


def _route(lg, norm_scale):
    n = lg.shape[1]
    big = jnp.float32(-3e38)
    g = [lg[i:i + 1] for i in range(N_GROUPS)]
    gmax, gidx = _first_max(g)
    gsum = jnp.exp(g[0] - gmax)
    for v in g[1:]:
        gsum = gsum + jnp.exp(v - gmax)
    gw = 1.0 / gsum
    e = []
    for j in range(EXPERTS_PER_GROUP):
        v = lg[N_GROUPS + j:N_GROUPS + j + 1]
        for gg in range(1, N_GROUPS):
            row = N_GROUPS + EXPERTS_PER_GROUP * gg + j
            v = jnp.where(gidx == gg, lg[row:row + 1], v)
        e.append(v)
    v1, i1 = _first_max(e)
    v2, i2 = _first_max([jnp.where(i1 == j, big, e[j]) for j in range(EXPERTS_PER_GROUP)])
    d = jnp.exp(v2 - v1)
    w1 = gw * (1.0 / (1.0 + d))
    w2 = gw * (d / (1.0 + d))
    first_lo = i1 < i2
    lo = jnp.where(first_lo, i1, i2)
    hi = jnp.where(first_lo, i2, i1)
    w_lo = jnp.where(first_lo, w1, w2)
    w_hi = jnp.where(first_lo, w2, w1)
    cls = 6.0 * gidx + 0.5 * (lo * (7.0 - lo)) + (hi - lo - 1.0)
    return jnp.concatenate([w_lo, w_hi, cls, norm_scale, jnp.zeros((SUBLANES - 4, n), F32)], axis=0)


def _out_kernel(x_tiled, a_ref, p_ref, x_ref, wa_ref, wp_ref, g_ref, wr_ref, x2_ref, r_ref, wa_bf, wp_bf):
    @pl.when(pl.program_id(0) == 0)
    def _():
        wa_bf[...] = wa_ref[0].astype(BF16)
        wp_bf[...] = wp_ref[0].astype(BF16)

    tm = a_ref.shape[0]
    n = tm // OUT_SPLIT
    nt_dims = (((1,), (1,)), ((), ()))
    for c in range(OUT_SPLIT):
        sl = pl.ds(c * n, n)
        tiles = pl.ds(c * n * TOKEN_ROWS, n * TOKEN_ROWS)
        mix = jnp.dot(a_ref[sl, :], wa_bf[...], preferred_element_type=F32)
        mix = mix + jnp.dot(p_ref[sl, :], wp_bf[...], preferred_element_type=F32)
        x2 = (_load_token_tiles(x_ref.at[tiles], n) if x_tiled else x_ref[sl, :]) + mix
        _store_token_tiles(x2_ref.at[tiles], x2)
        scale = _rms_scale(x2)
        scale_row = jnp.transpose(jnp.broadcast_to(scale, (n, LANES)))[:1]
        h = (x2 * scale) * g_ref[0]
        hh = h.astype(BF16)
        hl = (h - hh.astype(F32)).astype(BF16)
        r1 = lax.dot_general(wr_ref[0], hh, nt_dims, preferred_element_type=F32)
        r2 = lax.dot_general(wr_ref[0, :ROUTER_ROWS], hl, nt_dims, preferred_element_type=F32)
        r_ref[0, :, sl] = _route(r1[:ROUTER_ROWS] + r1[ROUTER_ROWS:] + r2, scale_row)


def _out_proj(a, p, x, x_tiled, w_out, g, wr, layer):
    t = a.shape[0]
    tm = TM_OUT
    row = lambda i: (i, 0)
    fixed = lambda i: (layer, 0, 0)
    return pl.pallas_call(
        functools.partial(_out_kernel, x_tiled),
        grid=(t // tm,),
        in_specs=[
            pl.BlockSpec((tm, D_ATTN), row),
            pl.BlockSpec((tm, D_POOL), row),
            _x_spec(tm, x_tiled),
            pl.BlockSpec((1, D_ATTN, D_MODEL), fixed),
            pl.BlockSpec((1, D_POOL, D_MODEL), lambda i: (layer, 1, 0)),
            pl.BlockSpec((1, 1, D_MODEL), fixed),
            pl.BlockSpec((1, 2 * ROUTER_ROWS, D_MODEL), fixed),
        ],
        out_specs=[_x_spec(tm, True),
                   pl.BlockSpec((1, SUBLANES, tm), lambda i: (i, 0, 0))],
        out_shape=[jax.ShapeDtypeStruct((t * TOKEN_ROWS, LANES), F32),
                   jax.ShapeDtypeStruct((t // tm, SUBLANES, tm), F32)],
        scratch_shapes=[pltpu.VMEM((D_ATTN, D_MODEL), BF16), pltpu.VMEM((D_POOL, D_MODEL), BF16)],
        compiler_params=_params(),
        name="out_proj",
    )(a, p, x, w_out, w_out, g, wr)


def _moe_kernel(tok_ref, tile_ref, first_ref, last_ref, lo_ref, hi_ref, newc_ref, elo_ref, ehi_ref,
                h_hbm, w_ref, g_ref, wg_lo, wu_lo, wd_lo, wg_hi, wu_hi, wd_hi, y_hbm,
                xbuf, ybuf, wg_bf, wu_bf, wd_bf, gsem, ssem):
    del elo_ref, ehi_ref
    k = pl.program_id(0)
    tm = xbuf.shape[1] // TOKEN_ROWS
    n_tiles = y_hbm.shape[0] // (tm * TOKEN_ROWS)
    tile = tile_ref[k]
    slot = tile % 2

    @pl.when(newc_ref[k] == 1)
    def _():
        for e, (wg, wu, wd) in enumerate(((wg_lo, wu_lo, wd_lo), (wg_hi, wu_hi, wd_hi))):
            wg_bf[e] = wg[0].astype(BF16)
            wu_bf[e] = wu[0].astype(BF16)
            wd_bf[e] = wd[0].astype(BF16)

    def token_rows(idx):
        return pl.ds(pl.multiple_of(idx * TOKEN_ROWS, TOKEN_ROWS), TOKEN_ROWS)

    def gather_copy(j, tok, sl):
        return pltpu.make_async_copy(h_hbm.at[token_rows(tok)], xbuf.at[sl, token_rows(j)], gsem.at[sl])

    def scatter_copy(j, tok, sl):
        return pltpu.make_async_copy(ybuf.at[sl, token_rows(j)], y_hbm.at[token_rows(tok)], ssem.at[sl])

    def start_rows(copy, tl, sl):
        def body(j, c):
            copy(j, tok_ref[tl * tm + j], sl).start()
            return c
        lax.fori_loop(0, tm, body, 0, unroll=16)

    def wait_rows(copy, sl):
        for j in range(tm):
            copy(j, 0, sl).wait()

    @pl.when(first_ref[k] == 1)
    def _():
        @pl.when(k == 0)
        def _():
            start_rows(gather_copy, tile, slot)
        wait_rows(gather_copy, slot)

        @pl.when(tile + 1 < n_tiles)
        def _():
            start_rows(gather_copy, tile + 1, 1 - slot)

        @pl.when(tile >= 2)
        def _():
            wait_rows(scatter_copy, slot)

    lo = lo_ref[k]
    hi = hi_ref[k]

    @pl.when(hi > lo)
    def _():
        xs = xbuf.at[slot]
        ys = ybuf.at[slot]
        w = w_ref[...]
        x = ((_load_token_tiles(xs, tm) * w[:, 2:3]) * g_ref[0]).astype(BF16)
        row = lax.broadcasted_iota(I32, (tm, 1), 0)
        mine = (row >= lo) & (row < hi)
        w_lo = jnp.where(mine, w[:, 0:1], 0.0)
        w_hi = jnp.where(mine, w[:, 1:2], 0.0)

        def expert(e):
            a = jax.nn.silu(jnp.dot(x, wg_bf[e], preferred_element_type=F32)) * jnp.dot(x, wu_bf[e], preferred_element_type=F32)
            return jnp.dot(a.astype(BF16), wd_bf[e], preferred_element_type=F32)

        y = w_lo * expert(0) + w_hi * expert(1)

        @pl.when(lo == 0)
        def _():
            for s in range(TOKEN_ROWS):
                rows_s = pl.ds(s, tm, stride=TOKEN_ROWS)
                ys[rows_s, :] = xs[rows_s, :] + y[:, s * LANES:(s + 1) * LANES]

        @pl.when(lo > 0)
        def _():
            for s in range(TOKEN_ROWS):
                rows_s = pl.ds(s, tm, stride=TOKEN_ROWS)
                ys[rows_s, :] = ys[rows_s, :] + y[:, s * LANES:(s + 1) * LANES]

    @pl.when(last_ref[k] == 1)
    def _():
        start_rows(scatter_copy, tile, slot)

    @pl.when(k == pl.num_programs(0) - 1)
    def _():
        wait_rows(scatter_copy, slot)
        if n_tiles > 1:
            wait_rows(scatter_copy, 1 - slot)


def _moe_tables(routing, n_tok, tm, expert0):
    w_lo = routing[:, 0, :].reshape(n_tok)
    w_hi = routing[:, 1, :].reshape(n_tok)
    cls = routing[:, 2, :].reshape(n_tok).astype(I32)
    norm_scale = routing[:, 3, :].reshape(n_tok)
    key = cls * n_tok + jnp.arange(n_tok, dtype=I32)
    skey, w_lo, w_hi, norm_scale = lax.sort((key, w_lo, w_hi, norm_scale), num_keys=1)
    tok_sorted = skey % n_tok
    bounds = jnp.arange(N_CLASSES + 1, dtype=I32) * n_tok
    cstart = jnp.sum((skey[None, :] < bounds[:, None]).astype(I32), axis=1)
    n_tiles = n_tok // tm
    cuts = jnp.sort(jnp.concatenate([jnp.arange(n_tiles, dtype=I32) * tm, cstart[1:N_CLASSES]]))
    ends = jnp.concatenate([cuts[1:], jnp.full((1,), n_tok, I32)])
    tile = jnp.minimum(cuts // tm, n_tiles - 1)
    c = jnp.clip(jnp.sum((cstart[None, :] <= cuts[:, None]).astype(I32), axis=1) - 1, 0, N_CLASSES - 1)
    lo = cuts - tile * tm
    hi = ends - tile * tm
    change = (tile[1:] != tile[:-1]).astype(I32)
    one = jnp.ones((1,), I32)
    first = jnp.concatenate([one, change])
    last = jnp.concatenate([change, one])
    new_class = jnp.concatenate([one, (c[1:] != c[:-1]).astype(I32)])
    grp = c // 6
    e_lo = expert0 + EXPERTS_PER_GROUP * grp + jnp.asarray(PAIR_LO)[c % 6]
    e_hi = expert0 + EXPERTS_PER_GROUP * grp + jnp.asarray(PAIR_HI)[c % 6]
    scalars = tuple(v.astype(I32) for v in (tok_sorted, tile, first, last, lo, hi, new_class, e_lo, e_hi))
    return scalars, jnp.stack([w_lo, w_hi, norm_scale], axis=1)


def _moe(x_tiles, routing, g, w_gate, w_up, w_down, layer):
    n_tok = x_tiles.shape[0] // TOKEN_ROWS
    tm = TM_MOE
    scalars, w_pair = _moe_tables(routing, n_tok, tm, layer * N_GROUPS * EXPERTS_PER_GROUP)
    n_steps = scalars[1].shape[0]
    by_tile = lambda k, tok, tile, first, last, lo, hi, newc, elo, ehi: (tile[k], 0)
    e_lo = lambda k, tok, tile, first, last, lo, hi, newc, elo, ehi: (elo[k], 0, 0)
    e_hi = lambda k, tok, tile, first, last, lo, hi, newc, elo, ehi: (ehi[k], 0, 0)
    up_blk = (1, D_MODEL, D_EXPERT)
    down_blk = (1, D_EXPERT, D_MODEL)
    grid_spec = pltpu.PrefetchScalarGridSpec(
        num_scalar_prefetch=len(scalars),
        grid=(n_steps,),
        in_specs=[
            pl.BlockSpec(memory_space=pl.ANY),
            pl.BlockSpec((tm, 3), by_tile),
            pl.BlockSpec((1, 1, D_MODEL), lambda k, *_: (layer, 0, 0)),
            pl.BlockSpec(up_blk, e_lo), pl.BlockSpec(up_blk, e_lo), pl.BlockSpec(down_blk, e_lo),
            pl.BlockSpec(up_blk, e_hi), pl.BlockSpec(up_blk, e_hi), pl.BlockSpec(down_blk, e_hi),
        ],
        out_specs=pl.BlockSpec(memory_space=pl.ANY),
        scratch_shapes=[
            pltpu.VMEM((2, tm * TOKEN_ROWS, LANES), F32),
            pltpu.VMEM((2, tm * TOKEN_ROWS, LANES), F32),
            pltpu.VMEM((2, D_MODEL, D_EXPERT), BF16),
            pltpu.VMEM((2, D_MODEL, D_EXPERT), BF16),
            pltpu.VMEM((2, D_EXPERT, D_MODEL), BF16),
            pltpu.SemaphoreType.DMA((2,)),
            pltpu.SemaphoreType.DMA((2,)),
        ],
    )
    return pl.pallas_call(
        _moe_kernel,
        grid_spec=grid_spec,
        out_shape=jax.ShapeDtypeStruct((n_tok * TOKEN_ROWS, LANES), F32),
        compiler_params=_params(),
        name="moe",
    )(*scalars, x_tiles, w_pair, g, w_gate, w_up, w_down, w_gate, w_up, w_down)


def _final_kernel(x_ref, g_ref, o_ref):
    x = _load_token_tiles(x_ref, o_ref.shape[0])
    o_ref[...] = (x * _rms_scale(x)) * g_ref[...]


def _final_norm(x_tiles, g):
    t = x_tiles.shape[0] // TOKEN_ROWS
    tm = TM_OUT
    return pl.pallas_call(
        _final_kernel,
        grid=(t // tm,),
        in_specs=[_x_spec(tm, True), pl.BlockSpec((1, D_MODEL), lambda i: (0, 0))],
        out_specs=_x_spec(tm, False),
        out_shape=jax.ShapeDtypeStruct((t, D_MODEL), F32),
        compiler_params=_params(),
        name="final_norm",
    )(x_tiles, g)


def _split_bf16(w):
    hi = w.astype(BF16)
    lo = (w - hi.astype(F32)).astype(BF16)
    return hi, lo


def kernel(x, norm_mix_g, w_in, rpb, pool_w, pool_scale, w_out, norm_ffn_g,
           w_router_group, w_router_expert, w_gate, w_up, w_down, final_g):
    batch, seq, d = x.shape
    depth = w_in.shape[0]
    assert d == D_MODEL and seq == GRID_H * GRID_W
    n_tok = batch * seq
    assert n_tok % TM_DENSE == 0 and n_tok % TM_OUT == 0 and n_tok % TM_MOE == 0

    bias = _attention_bias(rpb)
    cnt = jnp.asarray(_pool_counts(seq))
    n_pool = D_POOL // POOL_GROUP_DIM
    pool_scale4 = pool_scale.reshape(depth, n_pool, 1, POOL_GROUP_DIM)
    g_mix = norm_mix_g.reshape(depth, 1, D_MODEL)
    g_ffn = norm_ffn_g.reshape(depth, 1, D_MODEL)
    n_exp = N_GROUPS * EXPERTS_PER_GROUP
    w_gate_s = w_gate.reshape(depth * n_exp, D_MODEL, D_EXPERT)
    w_up_s = w_up.reshape(depth * n_exp, D_MODEL, D_EXPERT)
    w_down_s = w_down.reshape(depth * n_exp, D_EXPERT, D_MODEL)
    w_router = jnp.concatenate([w_router_group, w_router_expert], axis=-1).transpose(0, 2, 1)
    w_router = jnp.pad(w_router, ((0, 0), (0, ROUTER_ROWS - w_router.shape[1]), (0, 0)))
    w_router_b = jnp.concatenate(_split_bf16(w_router), axis=1)

    xs = x.reshape(n_tok, D_MODEL)
    for l in range(depth):
        tiled = l > 0
        proj = _in_proj(xs, tiled, g_mix, w_in, l, n_tok)
        a, p = _mixer(proj, bias, cnt, pool_w, pool_scale4, l, batch, seq)
        x_mid, routing = _out_proj(a, p, xs, tiled, w_out, g_ffn, w_router_b, l)
        xs = _moe(x_mid, routing, g_ffn, w_gate_s, w_up_s, w_down_s, l)
    out = _final_norm(xs, final_g.reshape(1, D_MODEL))
    return out.reshape(batch, seq, D_MODEL)
```

```python
import functools

import numpy as np
import jax
import jax.numpy as jnp
from jax import lax
from jax.experimental import pallas as pl
from jax.experimental.pallas import tpu as pltpu

F32 = jnp.float32
BF16 = jnp.bfloat16
I32 = jnp.int32

D_MODEL = 1024
GRID_W = 64
GRID_H = 32
D_ATTN = 512
HEAD_DIM = 64
N_HEADS = 8
WIN_H = 8
WIN_W = 16
D_POOL = 512
POOL_GROUP_DIM = 128
D_IN = 2048
N_GROUPS = 4
EXPERTS_PER_GROUP = 4
D_EXPERT = 512
N_CLASSES = 24
EPS = 1e-6
NEG_INF = -1e30
LOG2_E = 1.4426950408889634

LANES = 128
SUBLANES = 8
TOKEN_ROWS = D_MODEL // LANES
COL_BLOCK = 256
HEADS_PER_BLOCK = COL_BLOCK // HEAD_DIM
GROUPS_PER_BLOCK = COL_BLOCK // POOL_GROUP_DIM
N_COL_BLOCKS = D_ATTN // COL_BLOCK
Q_ROWS = 8
Q_COLS = 16
K_ROWS = 16
K_COLS = 32
Q_ROT = 8
ROW_BLOCK_KEY_START = (0, 4, 12, 16)
POOL_HALO = 16
POOL_WINDOWS = (2, 4, 8, 16)

TM_DENSE = 512
TM_OUT = 1024
TM_MOE = 256
VMEM_LIMIT = 56 * 1024 * 1024

PAIR_LO = np.array([0, 0, 0, 1, 1, 2], np.int32)
PAIR_HI = np.array([1, 2, 3, 2, 3, 3], np.int32)


def _params(n_axes=1):
    return pltpu.CompilerParams(dimension_semantics=("arbitrary",) * n_axes, vmem_limit_bytes=VMEM_LIMIT)


def _rms_scale(x):
    return lax.rsqrt(jnp.mean(x * x, axis=-1, keepdims=True) + EPS)


def _load_token_tiles(ref, n_tok):
    return jnp.concatenate([ref[pl.ds(s, n_tok, stride=TOKEN_ROWS), :] for s in range(TOKEN_ROWS)], axis=1)


def _store_token_tiles(ref, val):
    n_tok = val.shape[0]
    for s in range(TOKEN_ROWS):
        ref[pl.ds(s, n_tok, stride=TOKEN_ROWS), :] = val[:, s * LANES:(s + 1) * LANES]


def _x_spec(tm, tiled):
    shape = (tm * TOKEN_ROWS, LANES) if tiled else (tm, D_MODEL)
    return pl.BlockSpec(shape, lambda i: (i, 0))


def _in_kernel(x_tiled, x_ref, g_ref, w_ref, proj_ref, w_bf):
    @pl.when(pl.program_id(0) == 0)
    def _():
        w_bf[...] = w_ref[0].astype(BF16)

    tm = proj_ref.shape[0]
    x = _load_token_tiles(x_ref, tm) if x_tiled else x_ref[...]
    h = ((x * _rms_scale(x)) * g_ref[0]).astype(BF16)
    proj = jnp.dot(h, w_bf[...], preferred_element_type=F32)
    q = (proj[:, :D_ATTN] * (HEAD_DIM ** -0.5 * LOG2_E)).reshape(tm // GRID_W, GRID_W, D_ATTN)
    q = jnp.concatenate([q[:, Q_ROT:], q[:, :Q_ROT]], axis=1).reshape(tm, D_ATTN)
    proj_ref[:, :D_ATTN] = q.astype(BF16)
    proj_ref[:, D_ATTN:] = proj[:, D_ATTN:].astype(BF16)


def _in_proj(x, x_tiled, g, w, layer, n_tok):
    tm = TM_DENSE
    fixed = lambda i: (layer, 0, 0)
    return pl.pallas_call(
        functools.partial(_in_kernel, x_tiled),
        grid=(n_tok // tm,),
        in_specs=[_x_spec(tm, x_tiled), pl.BlockSpec((1, 1, D_MODEL), fixed), pl.BlockSpec((1, D_MODEL, D_IN), fixed)],
        out_specs=pl.BlockSpec((tm, D_IN), lambda i: (i, 0)),
        out_shape=jax.ShapeDtypeStruct((n_tok, D_IN), BF16),
        scratch_shapes=[pltpu.VMEM((D_MODEL, D_IN), BF16)],
        compiler_params=_params(),
        name="in_proj",
    )(x, g, w)


def _key_col_segments(m):
    if m < 3:
        return ((Q_COLS * m, K_COLS),)
    return ((0, K_COLS // 2), (GRID_W - K_COLS // 2, K_COLS // 2))


def _key_col(m, kk):
    for start, n in _key_col_segments(m):
        if kk < n:
            return start + kk
        kk -= n
    raise ValueError(kk)


def _attention_index_tables():
    dr = np.zeros((3, Q_ROWS, K_ROWS), np.int32)
    rv = np.zeros((3, Q_ROWS, K_ROWS), bool)
    for typ, rb in enumerate((0, 1, 3)):
        for i in range(Q_ROWS):
            r = Q_ROWS * rb + i
            r_start = min(max(r - WIN_H // 2, 0), GRID_H - WIN_H)
            for kr in range(K_ROWS):
                ka = ROW_BLOCK_KEY_START[rb] + kr
                rv[typ, i, kr] = r_start <= ka < r_start + WIN_H
                dr[typ, i, kr] = min(max(ka - r + WIN_H - 1, 0), 2 * WIN_H - 2)
    dc = np.zeros((2, Q_COLS, K_COLS), np.int32)
    cv = np.zeros((2, Q_COLS, K_COLS), bool)
    for mt, m in enumerate((0, 3)):
        for j in range(Q_COLS):
            qc = (Q_COLS * m + j + Q_ROT) % GRID_W
            c_start = min(max(qc - WIN_W // 2, 0), GRID_W - WIN_W)
            for kk in range(K_COLS):
                kc = _key_col(m, kk)
                cv[mt, j, kk] = c_start <= kc < c_start + WIN_W
                dc[mt, j, kk] = min(max(kc - qc, -(WIN_W - 1)), WIN_W - 1) + WIN_W - 1
    return dr, rv, dc, cv


KEY_ROWS_PER_LANE_TILE = LANES // K_COLS


def _bias_kernel(row_index, e_ref, o_ref):
    for typ in range(row_index.shape[0]):
        for mt in range(2):
            for i in range(Q_ROWS):
                for c in range(K_ROWS // KEY_ROWS_PER_LANE_TILE):
                    tile = None
                    for s in range(KEY_ROWS_PER_LANE_TILE):
                        piece = e_ref[0, int(row_index[typ, i, KEY_ROWS_PER_LANE_TILE * c + s]), mt, s]
                        tile = piece if tile is None else tile + piece
                    o_ref[0, 2 * typ + mt, pl.ds(Q_COLS * i, Q_COLS), pl.ds(LANES * c, LANES)] = tile


def _attention_bias(rpb):
    dr, rv, dc, cv = _attention_index_tables()
    nl, nh, n_dr, _ = rpb.shape
    e = jnp.where(jnp.asarray(cv)[None, None, None], rpb[:, :, :, dc].astype(F32) * LOG2_E, NEG_INF)
    e = jnp.concatenate([e, jnp.full((nl, nh, 1) + e.shape[3:], NEG_INF, F32)], axis=2)
    slots = [jnp.pad(e, ((0, 0),) * 5 + ((K_COLS * s, LANES - K_COLS * (s + 1)),))
             for s in range(KEY_ROWS_PER_LANE_TILE)]
    e4 = jnp.stack(slots, axis=4).reshape(nl * nh, n_dr + 1, 2, KEY_ROWS_PER_LANE_TILE, Q_COLS, LANES)
    row_index = np.where(rv, dr, n_dr)
    nq, nk = Q_ROWS * Q_COLS, K_ROWS * K_COLS
    return pl.pallas_call(
        functools.partial(_bias_kernel, row_index),
        grid=(nl * nh,),
        in_specs=[pl.BlockSpec((1,) + e4.shape[1:], lambda i: (i, 0, 0, 0, 0, 0))],
        out_specs=pl.BlockSpec((1, 6, nq, nk), lambda i: (i, 0, 0, 0)),
        out_shape=jax.ShapeDtypeStruct((nl * nh, 6, nq, nk), F32),
        compiler_params=_params(),
        name="attention_bias",
    )(e4)


def _pool_counts(seq):
    t = np.arange(seq)
    cols = []
    for w in POOL_WINDOWS:
        cnt = np.minimum(t + w // 2, seq) - np.maximum(t - w // 2, 0)
        cols.append(np.broadcast_to(cnt[:, None].astype(np.float32), (seq, POOL_GROUP_DIM)))
    return np.concatenate(cols, axis=1).reshape(seq, N_COL_BLOCKS, COL_BLOCK).transpose(1, 0, 2)


def _window_sum(ue, window):
    n = ue.shape[0]

    def shifted(x, d):
        return pltpu.roll(x, d % n, axis=0)

    s = ue + shifted(ue, 1)
    reach = 1
    while 2 * reach < window:
        s = shifted(s, reach) + shifted(s, -reach)
        reach *= 2
    return s


def _mix_kernel(q_ref, k_ref, v_ref, u_ref, bias_ref, cnt_ref, pw_ref, ps_ref, a_ref, p_ref):
    jb = pl.program_id(0)
    seq = q_ref.shape[1]
    nq = Q_ROWS * Q_COLS
    nk = K_ROWS * K_COLS
    lane_head = lax.broadcasted_iota(I32, (nq, COL_BLOCK), 1) // HEAD_DIM

    def rows(ref, grid_row, col, n):
        return ref[0, pl.ds(pl.multiple_of(grid_row * GRID_W + col, Q_COLS), n), :]

    def row_block(rb, carry):
        key_row0 = jnp.clip(Q_ROWS * rb - WIN_H // 2, 0, GRID_H - K_ROWS)
        typ = jnp.where(rb == 0, 0, jnp.where(rb == GRID_H // Q_ROWS - 1, 2, 1))
        outs = []
        for m in range(GRID_W // Q_COLS):
            mt = 0 if m < 3 else 1
            qb = jnp.concatenate([rows(q_ref, Q_ROWS * rb + i, Q_COLS * m, Q_COLS) for i in range(Q_ROWS)], axis=0)
            segs = _key_col_segments(m)
            kb = jnp.concatenate(
                [rows(k_ref, key_row0 + kr, c0, n) for kr in range(K_ROWS) for c0, n in segs], axis=0)
            vb = jnp.concatenate(
                [rows(v_ref, key_row0 + kr, c0, n) for kr in range(K_ROWS) for c0, n in segs], axis=0)
            zero = jnp.zeros_like(qb)
            qh = jnp.concatenate([jnp.where(lane_head == h, qb, zero) for h in range(HEADS_PER_BLOCK)], axis=0)
            s = lax.dot_general(qh, kb, (((1,), (1,)), ((), ())), preferred_element_type=F32)
            s = s + jnp.concatenate([bias_ref[h, 2 * typ + mt] for h in range(HEADS_PER_BLOCK)], axis=0)
            mx = jnp.max(s, axis=-1, keepdims=True)
            e = jnp.exp2(s - mx)
            den = jnp.sum(e, axis=-1, keepdims=True)
            oh = jnp.dot(e.astype(BF16), vb, preferred_element_type=F32)
            o = oh[:nq]
            dn = jnp.broadcast_to(den[:nq], (nq, COL_BLOCK))
            for h in range(1, HEADS_PER_BLOCK):
                o = jnp.where(lane_head == h, oh[h * nq:(h + 1) * nq], o)
                dn = jnp.where(lane_head == h, den[h * nq:(h + 1) * nq], dn)
            outs.append(o / dn)
        for i in range(Q_ROWS):
            pieces = []
            for m in range(GRID_W // Q_COLS):
                pieces.append(outs[(m - 1) % 4][Q_COLS * i + Q_ROT:Q_COLS * (i + 1)])
                pieces.append(outs[m][Q_COLS * i:Q_COLS * i + Q_ROT])
            start = pl.multiple_of((Q_ROWS * rb + i) * GRID_W, GRID_W)
            a_ref[0, pl.ds(start, GRID_W), :] = jnp.concatenate(pieces, axis=0).astype(BF16)
        return carry

    lax.fori_loop(0, GRID_H // Q_ROWS, row_block, 0)

    def pool(windows):
        u = u_ref[0].astype(F32)
        pad = jnp.zeros((POOL_HALO, POOL_GROUP_DIM), F32)
        for gi, window in enumerate(windows):
            sl = slice(gi * POOL_GROUP_DIM, (gi + 1) * POOL_GROUP_DIM)
            ug = u[:, sl]
            win = _window_sum(jnp.concatenate([pad, ug, pad], axis=0), window)[POOL_HALO:POOL_HALO + seq]
            pooled = win / cnt_ref[0, :, sl] - ug
            mixed = jnp.dot(pooled.astype(BF16), pw_ref[0, gi].astype(BF16), preferred_element_type=F32) * ps_ref[0, gi]
            p_ref[0, :, sl] = mixed.astype(BF16)

    for cb in range(N_COL_BLOCKS):
        @pl.when(jb == cb)
        def _(cb=cb):
            pool(POOL_WINDOWS[cb * GROUPS_PER_BLOCK:(cb + 1) * GROUPS_PER_BLOCK])


def _mixer(proj, bias, cnt, pool_w, pool_scale, layer, batch, seq):
    proj3 = proj.reshape(batch, seq, D_IN)
    col = lambda base: (lambda j, b: (b, 0, base + j))
    blk = (1, seq, COL_BLOCK)
    bias_blk = (HEADS_PER_BLOCK,) + bias.shape[1:]
    a, p = pl.pallas_call(
        _mix_kernel,
        grid=(N_COL_BLOCKS, batch),
        in_specs=[
            pl.BlockSpec(blk, col(0)),
            pl.BlockSpec(blk, col(N_COL_BLOCKS)),
            pl.BlockSpec(blk, col(2 * N_COL_BLOCKS)),
            pl.BlockSpec(blk, col(3 * N_COL_BLOCKS)),
            pl.BlockSpec(bias_blk, lambda j, b: (layer * N_COL_BLOCKS + j, 0, 0, 0)),
            pl.BlockSpec(blk, lambda j, b: (j, 0, 0)),
            pl.BlockSpec((1, GROUPS_PER_BLOCK, POOL_GROUP_DIM, POOL_GROUP_DIM), lambda j, b: (layer, j, 0, 0)),
            pl.BlockSpec((1, GROUPS_PER_BLOCK, 1, POOL_GROUP_DIM), lambda j, b: (layer, j, 0, 0)),
        ],
        out_specs=[pl.BlockSpec(blk, lambda j, b: (b, 0, j)), pl.BlockSpec(blk, lambda j, b: (b, 0, j))],
        out_shape=[jax.ShapeDtypeStruct((batch, seq, D_ATTN), BF16), jax.ShapeDtypeStruct((batch, seq, D_POOL), BF16)],
        compiler_params=_params(2),
        name="mixer",
    )(proj3, proj3, proj3, proj3, bias, cnt, pool_w, pool_scale)
    return a.reshape(batch * seq, D_ATTN), p.reshape(batch * seq, D_POOL)


ROUTER_ROWS = 32
OUT_SPLIT = 4


def _first_max(vals):
    best = vals[0]
    for v in vals[1:]:
        best = jnp.maximum(best, v)
    idx = jnp.float32(len(vals) - 1)
    for i in range(len(vals) - 2, -1, -1):
        idx = jnp.where(vals[i] == best, jnp.float32(i), idx)
    return best, idx


def _route(lg, norm_scale):
    n = lg.shape[1]
    big = jnp.float32(-3e38)
    g = [lg[i:i + 1] for i in range(N_GROUPS)]
    gmax, gidx = _first_max(g)
    gsum = jnp.exp(g[0] - gmax)
    for v in g[1:]:
        gsum = gsum + jnp.exp(v - gmax)
    gw = 1.0 / gsum
    e = []
    for j in range(EXPERTS_PER_GROUP):
        v = lg[N_GROUPS + j:N_GROUPS + j + 1]
        for gg in range(1, N_GROUPS):
            row = N_GROUPS + EXPERTS_PER_GROUP * gg + j
            v = jnp.where(gidx == gg, lg[row:row + 1], v)
        e.append(v)
    v1, i1 = _first_max(e)
    v2, i2 = _first_max([jnp.where(i1 == j, big, e[j]) for j in range(EXPERTS_PER_GROUP)])
    d = jnp.exp(v2 - v1)
    w1 = gw * (1.0 / (1.0 + d))
    w2 = gw * (d / (1.0 + d))
    first_lo = i1 < i2
    lo = jnp.where(first_lo, i1, i2)
    hi = jnp.where(first_lo, i2, i1)
    w_lo = jnp.where(first_lo, w1, w2)
    w_hi = jnp.where(first_lo, w2, w1)
    cls = 6.0 * gidx + 0.5 * (lo * (7.0 - lo)) + (hi - lo - 1.0)
    return jnp.concatenate([w_lo, w_hi, cls, norm_scale, jnp.zeros((SUBLANES - 4, n), F32)], axis=0)


def _out_kernel(x_tiled, a_ref, p_ref, x_ref, wa_ref, wp_ref, g_ref, wr_ref, x2_ref, r_ref, wa_bf, wp_bf):
    @pl.when(pl.program_id(0) == 0)
    def _():
        wa_bf[...] = wa_ref[0].astype(BF16)
        wp_bf[...] = wp_ref[0].astype(BF16)

    tm = a_ref.shape[0]
    n = tm // OUT_SPLIT
    nt_dims = (((1,), (1,)), ((), ()))
    for c in range(OUT_SPLIT):
        sl = pl.ds(c * n, n)
        tiles = pl.ds(c * n * TOKEN_ROWS, n * TOKEN_ROWS)
        mix = jnp.dot(a_ref[sl, :], wa_bf[...], preferred_element_type=F32)
        mix = mix + jnp.dot(p_ref[sl, :], wp_bf[...], preferred_element_type=F32)
        x2 = (_load_token_tiles(x_ref.at[tiles], n) if x_tiled else x_ref[sl, :]) + mix
        _store_token_tiles(x2_ref.at[tiles], x2)
        scale = _rms_scale(x2)
        scale_row = jnp.transpose(jnp.broadcast_to(scale, (n, LANES)))[:1]
        h = (x2 * scale) * g_ref[0]
        hh = h.astype(BF16)
        hl = (h - hh.astype(F32)).astype(BF16)
        r1 = lax.dot_general(wr_ref[0], hh, nt_dims, preferred_element_type=F32)
        r2 = lax.dot_general(wr_ref[0, :ROUTER_ROWS], hl, nt_dims, preferred_element_type=F32)
        r_ref[0, :, sl] = _route(r1[:ROUTER_ROWS] + r1[ROUTER_ROWS:] + r2, scale_row)


def _out_proj(a, p, x, x_tiled, w_out, g, wr, layer):
    t = a.shape[0]
    tm = TM_OUT
    row = lambda i: (i, 0)
    fixed = lambda i: (layer, 0, 0)
    return pl.pallas_call(
        functools.partial(_out_kernel, x_tiled),
        grid=(t // tm,),
        in_specs=[
            pl.BlockSpec((tm, D_ATTN), row),
            pl.BlockSpec((tm, D_POOL), row),
            _x_spec(tm, x_tiled),
            pl.BlockSpec((1, D_ATTN, D_MODEL), fixed),
            pl.BlockSpec((1, D_POOL, D_MODEL), lambda i: (layer, 1, 0)),
            pl.BlockSpec((1, 1, D_MODEL), fixed),
            pl.BlockSpec((1, 2 * ROUTER_ROWS, D_MODEL), fixed),
        ],
        out_specs=[_x_spec(tm, True),
                   pl.BlockSpec((1, SUBLANES, tm), lambda i: (i, 0, 0))],
        out_shape=[jax.ShapeDtypeStruct((t * TOKEN_ROWS, LANES), F32),
                   jax.ShapeDtypeStruct((t // tm, SUBLANES, tm), F32)],
        scratch_shapes=[pltpu.VMEM((D_ATTN, D_MODEL), BF16), pltpu.VMEM((D_POOL, D_MODEL), BF16)],
        compiler_params=_params(),
        name="out_proj",
    )(a, p, x, w_out, w_out, g, wr)


def _moe_kernel(tok_ref, tile_ref, first_ref, last_ref, lo_ref, hi_ref, newc_ref, elo_ref, ehi_ref,
                h_hbm, w_ref, g_ref, wg_lo, wu_lo, wd_lo, wg_hi, wu_hi, wd_hi, y_hbm,
                xbuf, ybuf, wg_bf, wu_bf, wd_bf, gsem, ssem):
    del elo_ref, ehi_ref
    k = pl.program_id(0)
    tm = xbuf.shape[1] // TOKEN_ROWS
    n_tiles = y_hbm.shape[0] // (tm * TOKEN_ROWS)
    tile = tile_ref[k]
    slot = tile % 2

    @pl.when(newc_ref[k] == 1)
    def _():
        for e, (wg, wu, wd) in enumerate(((wg_lo, wu_lo, wd_lo), (wg_hi, wu_hi, wd_hi))):
            wg_bf[e] = wg[0].astype(BF16)
            wu_bf[e] = wu[0].astype(BF16)
            wd_bf[e] = wd[0].astype(BF16)

    def token_rows(idx):
        return pl.ds(pl.multiple_of(idx * TOKEN_ROWS, TOKEN_ROWS), TOKEN_ROWS)

    def gather_copy(j, tok, sl):
        return pltpu.make_async_copy(h_hbm.at[token_rows(tok)], xbuf.at[sl, token_rows(j)], gsem.at[sl])

    def scatter_copy(j, tok, sl):
        return pltpu.make_async_copy(ybuf.at[sl, token_rows(j)], y_hbm.at[token_rows(tok)], ssem.at[sl])

    def start_rows(copy, tl, sl):
        def body(j, c):
            copy(j, tok_ref[tl * tm + j], sl).start()
            return c
        lax.fori_loop(0, tm, body, 0, unroll=16)

    def wait_rows(copy, sl):
        for j in range(tm):
            copy(j, 0, sl).wait()

    @pl.when(first_ref[k] == 1)
    def _():
        @pl.when(k == 0)
        def _():
            start_rows(gather_copy, tile, slot)
        wait_rows(gather_copy, slot)

        @pl.when(tile + 1 < n_tiles)
        def _():
            start_rows(gather_copy, tile + 1, 1 - slot)

        @pl.when(tile >= 2)
        def _():
            wait_rows(scatter_copy, slot)

    lo = lo_ref[k]
    hi = hi_ref[k]

    @pl.when(hi > lo)
    def _():
        xs = xbuf.at[slot]
        ys = ybuf.at[slot]
        w = jnp.transpose(jnp.concatenate([w_ref[0], jnp.zeros((LANES - SUBLANES, tm), F32)], axis=0))
        x = ((_load_token_tiles(xs, tm) * w[:, 2:3]) * g_ref[0]).astype(BF16)
        row = lax.broadcasted_iota(I32, (tm, 1), 0)
        mine = (row >= lo) & (row < hi)
        w_lo = jnp.where(mine, w[:, 0:1], 0.0)
        w_hi = jnp.where(mine, w[:, 1:2], 0.0)

        def expert(e):
            a = jax.nn.silu(jnp.dot(x, wg_bf[e], preferred_element_type=F32)) * jnp.dot(x, wu_bf[e], preferred_element_type=F32)
            return jnp.dot(a.astype(BF16), wd_bf[e], preferred_element_type=F32)

        y = w_lo * expert(0) + w_hi * expert(1)

        @pl.when(lo == 0)
        def _():
            for s in range(TOKEN_ROWS):
                rows_s = pl.ds(s, tm, stride=TOKEN_ROWS)
                ys[rows_s, :] = xs[rows_s, :] + y[:, s * LANES:(s + 1) * LANES]

        @pl.when(lo > 0)
        def _():
            for s in range(TOKEN_ROWS):
                rows_s = pl.ds(s, tm, stride=TOKEN_ROWS)
                ys[rows_s, :] = ys[rows_s, :] + y[:, s * LANES:(s + 1) * LANES]

    @pl.when(last_ref[k] == 1)
    def _():
        start_rows(scatter_copy, tile, slot)

    @pl.when(k == pl.num_programs(0) - 1)
    def _():
        wait_rows(scatter_copy, slot)
        if n_tiles > 1:
            wait_rows(scatter_copy, 1 - slot)


def _moe_tables(routing, n_tok, tm, expert0):
    w_lo = routing[:, 0, :].reshape(n_tok)
    w_hi = routing[:, 1, :].reshape(n_tok)
    cls = routing[:, 2, :].reshape(n_tok).astype(I32)
    norm_scale = routing[:, 3, :].reshape(n_tok)
    key = cls * n_tok + jnp.arange(n_tok, dtype=I32)
    skey, w_lo, w_hi, norm_scale = lax.sort((key, w_lo, w_hi, norm_scale), num_keys=1, is_stable=False)
    tok_sorted = skey % n_tok
    bounds = jnp.arange(N_CLASSES + 1, dtype=I32) * n_tok
    cstart = jnp.sum((skey[None, :] < bounds[:, None]).astype(I32), axis=1)
    n_tiles = n_tok // tm
    cuts = jnp.sort(jnp.concatenate([jnp.arange(n_tiles, dtype=I32) * tm, cstart[1:N_CLASSES]]))
    ends = jnp.concatenate([cuts[1:], jnp.full((1,), n_tok, I32)])
    tile = jnp.minimum(cuts // tm, n_tiles - 1)
    c = jnp.clip(jnp.sum((cstart[None, :] <= cuts[:, None]).astype(I32), axis=1) - 1, 0, N_CLASSES - 1)
    lo = cuts - tile * tm
    hi = ends - tile * tm
    change = (tile[1:] != tile[:-1]).astype(I32)
    one = jnp.ones((1,), I32)
    first = jnp.concatenate([one, change])
    last = jnp.concatenate([change, one])
    new_class = jnp.concatenate([one, (c[1:] != c[:-1]).astype(I32)])
    grp = c // 6
    e_lo = expert0 + EXPERTS_PER_GROUP * grp + jnp.asarray(PAIR_LO)[c % 6]
    e_hi = expert0 + EXPERTS_PER_GROUP * grp + jnp.asarray(PAIR_HI)[c % 6]
    scalars = tuple(v.astype(I32) for v in (tok_sorted, tile, first, last, lo, hi, new_class, e_lo, e_hi))
    rows = [v.reshape(n_tiles, 1, tm) for v in (w_lo, w_hi, norm_scale)]
    rows.append(jnp.zeros((n_tiles, SUBLANES - len(rows), tm), F32))
    return scalars, jnp.concatenate(rows, axis=1)


def _moe(x_tiles, routing, g, w_gate, w_up, w_down, layer):
    n_tok = x_tiles.shape[0] // TOKEN_ROWS
    tm = TM_MOE
    scalars, w_pair = _moe_tables(routing, n_tok, tm, layer * N_GROUPS * EXPERTS_PER_GROUP)
    n_steps = scalars[1].shape[0]
    by_tile = lambda k, tok, tile, first, last, lo, hi, newc, elo, ehi: (tile[k], 0, 0)
    e_lo = lambda k, tok, tile, first, last, lo, hi, newc, elo, ehi: (elo[k], 0, 0)
    e_hi = lambda k, tok, tile, first, last, lo, hi, newc, elo, ehi: (ehi[k], 0, 0)
    up_blk = (1, D_MODEL, D_EXPERT)
    down_blk = (1, D_EXPERT, D_MODEL)
    grid_spec = pltpu.PrefetchScalarGridSpec(
        num_scalar_prefetch=len(scalars),
        grid=(n_steps,),
        in_specs=[
            pl.BlockSpec(memory_space=pl.ANY),
            pl.BlockSpec((1, SUBLANES, tm), by_tile),
            pl.BlockSpec((1, 1, D_MODEL), lambda k, *_: (layer, 0, 0)),
            pl.BlockSpec(up_blk, e_lo), pl.BlockSpec(up_blk, e_lo), pl.BlockSpec(down_blk, e_lo),
            pl.BlockSpec(up_blk, e_hi), pl.BlockSpec(up_blk, e_hi), pl.BlockSpec(down_blk, e_hi),
        ],
        out_specs=pl.BlockSpec(memory_space=pl.ANY),
        scratch_shapes=[
            pltpu.VMEM((2, tm * TOKEN_ROWS, LANES), F32),
            pltpu.VMEM((2, tm * TOKEN_ROWS, LANES), F32),
            pltpu.VMEM((2, D_MODEL, D_EXPERT), BF16),
            pltpu.VMEM((2, D_MODEL, D_EXPERT), BF16),
            pltpu.VMEM((2, D_EXPERT, D_MODEL), BF16),
            pltpu.SemaphoreType.DMA((2,)),
            pltpu.SemaphoreType.DMA((2,)),
        ],
    )
    return pl.pallas_call(
        _moe_kernel,
        grid_spec=grid_spec,
        out_shape=jax.ShapeDtypeStruct((n_tok * TOKEN_ROWS, LANES), F32),
        compiler_params=_params(),
        name="moe",
    )(*scalars, x_tiles, w_pair, g, w_gate, w_up, w_down, w_gate, w_up, w_down)


def _final_kernel(x_ref, g_ref, o_ref):
    x = _load_token_tiles(x_ref, o_ref.shape[0])
    o_ref[...] = (x * _rms_scale(x)) * g_ref[...]


def _final_norm(x_tiles, g):
    t = x_tiles.shape[0] // TOKEN_ROWS
    tm = TM_OUT
    return pl.pallas_call(
        _final_kernel,
        grid=(t // tm,),
        in_specs=[_x_spec(tm, True), pl.BlockSpec((1, D_MODEL), lambda i: (0, 0))],
        out_specs=_x_spec(tm, False),
        out_shape=jax.ShapeDtypeStruct((t, D_MODEL), F32),
        compiler_params=_params(),
        name="final_norm",
    )(x_tiles, g)


def _split_bf16(w):
    hi = w.astype(BF16)
    lo = (w - hi.astype(F32)).astype(BF16)
    return hi, lo


def kernel(x, norm_mix_g, w_in, rpb, pool_w, pool_scale, w_out, norm_ffn_g,
           w_router_group, w_router_expert, w_gate, w_up, w_down, final_g):
    batch, seq, d = x.shape
    depth = w_in.shape[0]
    assert d == D_MODEL and seq == GRID_H * GRID_W
    n_tok = batch * seq
    assert n_tok % TM_DENSE == 0 and n_tok % TM_OUT == 0 and n_tok % TM_MOE == 0

    bias = _attention_bias(rpb)
    cnt = jnp.asarray(_pool_counts(seq))
    n_pool = D_POOL // POOL_GROUP_DIM
    pool_scale4 = pool_scale.reshape(depth, n_pool, 1, POOL_GROUP_DIM)
    g_mix = norm_mix_g.reshape(depth, 1, D_MODEL)
    g_ffn = norm_ffn_g.reshape(depth, 1, D_MODEL)
    n_exp = N_GROUPS * EXPERTS_PER_GROUP
    w_gate_s = w_gate.reshape(depth * n_exp, D_MODEL, D_EXPERT)
    w_up_s = w_up.reshape(depth * n_exp, D_MODEL, D_EXPERT)
    w_down_s = w_down.reshape(depth * n_exp, D_EXPERT, D_MODEL)
    w_router = jnp.concatenate([w_router_group, w_router_expert], axis=-1).transpose(0, 2, 1)
    w_router = jnp.pad(w_router, ((0, 0), (0, ROUTER_ROWS - w_router.shape[1]), (0, 0)))
    w_router_b = jnp.concatenate(_split_bf16(w_router), axis=1)

    xs = x.reshape(n_tok, D_MODEL)
    for l in range(depth):
        tiled = l > 0
        proj = _in_proj(xs, tiled, g_mix, w_in, l, n_tok)
        a, p = _mixer(proj, bias, cnt, pool_w, pool_scale4, l, batch, seq)
        x_mid, routing = _out_proj(a, p, xs, tiled, w_out, g_ffn, w_router_b, l)
        xs = _moe(x_mid, routing, g_ffn, w_gate_s, w_up_s, w_down_s, l)
    out = _final_norm(xs, final_g.reshape(1, D_MODEL))
    return out.reshape(batch, seq, D_MODEL)
```

```python
import functools

import numpy as np
import jax
import jax.numpy as jnp
from jax import lax
from jax.experimental import pallas as pl
from jax.experimental.pallas import tpu as pltpu

F32 = jnp.float32
BF16 = jnp.bfloat16
I32 = jnp.int32

D_MODEL = 1024
GRID_W = 64
GRID_H = 32
D_ATTN = 512
HEAD_DIM = 64
N_HEADS = 8
WIN_H = 8
WIN_W = 16
D_POOL = 512
POOL_GROUP_DIM = 128
D_IN = 2048
N_GROUPS = 4
EXPERTS_PER_GROUP = 4
D_EXPERT = 512
N_CLASSES = 24
EPS = 1e-6
NEG_INF = -1e30
LOG2_E = 1.4426950408889634

LANES = 128
SUBLANES = 8
TOKEN_ROWS = D_MODEL // LANES
COL_BLOCK = 256
HEADS_PER_BLOCK = COL_BLOCK // HEAD_DIM
GROUPS_PER_BLOCK = COL_BLOCK // POOL_GROUP_DIM
N_COL_BLOCKS = D_ATTN // COL_BLOCK
Q_ROWS = 8
Q_COLS = 16
K_ROWS = 16
K_COLS = 32
Q_ROT = 8
ROW_BLOCK_KEY_START = (0, 4, 12, 16)
POOL_HALO = 16
POOL_WINDOWS = (2, 4, 8, 16)

TM_DENSE = 512
TM_OUT = 1024
TM_MOE = 256
VMEM_LIMIT = 56 * 1024 * 1024

PAIR_LO = np.array([0, 0, 0, 1, 1, 2], np.int32)
PAIR_HI = np.array([1, 2, 3, 2, 3, 3], np.int32)


def _params(n_axes=1):
    return pltpu.CompilerParams(dimension_semantics=("arbitrary",) * n_axes, vmem_limit_bytes=VMEM_LIMIT)


def _rms_scale(x):
    return lax.rsqrt(jnp.mean(x * x, axis=-1, keepdims=True) + EPS)


def _load_token_tiles(ref, n_tok):
    return jnp.concatenate([ref[pl.ds(s, n_tok, stride=TOKEN_ROWS), :] for s in range(TOKEN_ROWS)], axis=1)


def _store_token_tiles(ref, val):
    n_tok = val.shape[0]
    for s in range(TOKEN_ROWS):
        ref[pl.ds(s, n_tok, stride=TOKEN_ROWS), :] = val[:, s * LANES:(s + 1) * LANES]


def _x_spec(tm, tiled):
    shape = (tm * TOKEN_ROWS, LANES) if tiled else (tm, D_MODEL)
    return pl.BlockSpec(shape, lambda i: (i, 0))


def _in_kernel(x_tiled, x_ref, g_ref, w_ref, proj_ref, w_bf):
    @pl.when(pl.program_id(0) == 0)
    def _():
        w_bf[...] = w_ref[0].astype(BF16)

    tm = proj_ref.shape[0]
    x = _load_token_tiles(x_ref, tm) if x_tiled else x_ref[...]
    h = ((x * _rms_scale(x)) * g_ref[0]).astype(BF16)
    proj = jnp.dot(h, w_bf[...], preferred_element_type=F32)
    q = (proj[:, :D_ATTN] * (HEAD_DIM ** -0.5 * LOG2_E)).reshape(tm // GRID_W, GRID_W, D_ATTN)
    q = jnp.concatenate([q[:, Q_ROT:], q[:, :Q_ROT]], axis=1).reshape(tm, D_ATTN)
    proj_ref[:, :D_ATTN] = q.astype(BF16)
    proj_ref[:, D_ATTN:] = proj[:, D_ATTN:].astype(BF16)


def _in_proj(x, x_tiled, g, w, layer, n_tok):
    tm = TM_DENSE
    fixed = lambda i: (layer, 0, 0)
    return pl.pallas_call(
        functools.partial(_in_kernel, x_tiled),
        grid=(n_tok // tm,),
        in_specs=[_x_spec(tm, x_tiled), pl.BlockSpec((1, 1, D_MODEL), fixed), pl.BlockSpec((1, D_MODEL, D_IN), fixed)],
        out_specs=pl.BlockSpec((tm, D_IN), lambda i: (i, 0)),
        out_shape=jax.ShapeDtypeStruct((n_tok, D_IN), BF16),
        scratch_shapes=[pltpu.VMEM((D_MODEL, D_IN), BF16)],
        compiler_params=_params(),
        name="in_proj",
    )(x, g, w)


def _key_col_segments(m):
    if m < 3:
        return ((Q_COLS * m, K_COLS),)
    return ((0, K_COLS // 2), (GRID_W - K_COLS // 2, K_COLS // 2))


def _key_col(m, kk):
    for start, n in _key_col_segments(m):
        if kk < n:
            return start + kk
        kk -= n
    raise ValueError(kk)


def _attention_index_tables():
    dr = np.zeros((3, Q_ROWS, K_ROWS), np.int32)
    rv = np.zeros((3, Q_ROWS, K_ROWS), bool)
    for typ, rb in enumerate((0, 1, 3)):
        for i in range(Q_ROWS):
            r = Q_ROWS * rb + i
            r_start = min(max(r - WIN_H // 2, 0), GRID_H - WIN_H)
            for kr in range(K_ROWS):
                ka = ROW_BLOCK_KEY_START[rb] + kr
                rv[typ, i, kr] = r_start <= ka < r_start + WIN_H
                dr[typ, i, kr] = min(max(ka - r + WIN_H - 1, 0), 2 * WIN_H - 2)
    dc = np.zeros((2, Q_COLS, K_COLS), np.int32)
    cv = np.zeros((2, Q_COLS, K_COLS), bool)
    for mt, m in enumerate((0, 3)):
        for j in range(Q_COLS):
            qc = (Q_COLS * m + j + Q_ROT) % GRID_W
            c_start = min(max(qc - WIN_W // 2, 0), GRID_W - WIN_W)
            for kk in range(K_COLS):
                kc = _key_col(m, kk)
                cv[mt, j, kk] = c_start <= kc < c_start + WIN_W
                dc[mt, j, kk] = min(max(kc - qc, -(WIN_W - 1)), WIN_W - 1) + WIN_W - 1
    return dr, rv, dc, cv


KEY_ROWS_PER_LANE_TILE = LANES // K_COLS


def _bias_kernel(row_index, e_ref, o_ref):
    for typ in range(row_index.shape[0]):
        for mt in range(2):
            for i in range(Q_ROWS):
                for c in range(K_ROWS // KEY_ROWS_PER_LANE_TILE):
                    tile = None
                    for s in range(KEY_ROWS_PER_LANE_TILE):
                        piece = e_ref[0, int(row_index[typ, i, KEY_ROWS_PER_LANE_TILE * c + s]), mt, s]
                        tile = piece if tile is None else tile + piece
                    o_ref[0, 2 * typ + mt, pl.ds(Q_COLS * i, Q_COLS), pl.ds(LANES * c, LANES)] = tile


def _attention_bias(rpb):
    dr, rv, dc, cv = _attention_index_tables()
    nl, nh, n_dr, _ = rpb.shape
    e = jnp.where(jnp.asarray(cv)[None, None, None], rpb[:, :, :, dc].astype(F32) * LOG2_E, NEG_INF)
    e = jnp.concatenate([e, jnp.full((nl, nh, 1) + e.shape[3:], NEG_INF, F32)], axis=2)
    slots = [jnp.pad(e, ((0, 0),) * 5 + ((K_COLS * s, LANES - K_COLS * (s + 1)),))
             for s in range(KEY_ROWS_PER_LANE_TILE)]
    e4 = jnp.stack(slots, axis=4).reshape(nl * nh, n_dr + 1, 2, KEY_ROWS_PER_LANE_TILE, Q_COLS, LANES)
    row_index = np.where(rv, dr, n_dr)
    nq, nk = Q_ROWS * Q_COLS, K_ROWS * K_COLS
    return pl.pallas_call(
        functools.partial(_bias_kernel, row_index),
        grid=(nl * nh,),
        in_specs=[pl.BlockSpec((1,) + e4.shape[1:], lambda i: (i, 0, 0, 0, 0, 0))],
        out_specs=pl.BlockSpec((1, 6, nq, nk), lambda i: (i, 0, 0, 0)),
        out_shape=jax.ShapeDtypeStruct((nl * nh, 6, nq, nk), F32),
        compiler_params=_params(),
        name="attention_bias",
    )(e4)


def _pool_counts(seq):
    t = np.arange(seq)
    cols = []
    for w in POOL_WINDOWS:
        cnt = np.minimum(t + w // 2, seq) - np.maximum(t - w // 2, 0)
        cols.append(np.broadcast_to(cnt[:, None].astype(np.float32), (seq, POOL_GROUP_DIM)))
    return np.concatenate(cols, axis=1).reshape(seq, N_COL_BLOCKS, COL_BLOCK).transpose(1, 0, 2)


def _window_sum(ue, window):
    n = ue.shape[0]

    def shifted(x, d):
        return pltpu.roll(x, d % n, axis=0)

    s = ue + shifted(ue, 1)
    reach = 1
    while 2 * reach < window:
        s = shifted(s, reach) + shifted(s, -reach)
        reach *= 2
    return s


def _mix_kernel(q_ref, k_ref, v_ref, u_ref, bias_ref, cnt_ref, pw_ref, ps_ref, a_ref, p_ref):
    jb = pl.program_id(0)
    seq = q_ref.shape[1]
    nq = Q_ROWS * Q_COLS
    nk = K_ROWS * K_COLS
    lane_head = lax.broadcasted_iota(I32, (nq, COL_BLOCK), 1) // HEAD_DIM

    def rows(ref, grid_row, col, n):
        return ref[0, pl.ds(pl.multiple_of(grid_row * GRID_W + col, Q_COLS), n), :]

    def row_block(rb, carry):
        key_row0 = jnp.clip(Q_ROWS * rb - WIN_H // 2, 0, GRID_H - K_ROWS)
        typ = jnp.where(rb == 0, 0, jnp.where(rb == GRID_H // Q_ROWS - 1, 2, 1))
        outs = []
        for m in range(GRID_W // Q_COLS):
            mt = 0 if m < 3 else 1
            qb = jnp.concatenate([rows(q_ref, Q_ROWS * rb + i, Q_COLS * m, Q_COLS) for i in range(Q_ROWS)], axis=0)
            segs = _key_col_segments(m)
            kb = jnp.concatenate(
                [rows(k_ref, key_row0 + kr, c0, n) for kr in range(K_ROWS) for c0, n in segs], axis=0)
            vb = jnp.concatenate(
                [rows(v_ref, key_row0 + kr, c0, n) for kr in range(K_ROWS) for c0, n in segs], axis=0)
            zero = jnp.zeros_like(qb)
            qh = jnp.concatenate([jnp.where(lane_head == h, qb, zero) for h in range(HEADS_PER_BLOCK)], axis=0)
            s = lax.dot_general(qh, kb, (((1,), (1,)), ((), ())), preferred_element_type=F32)
            s = s + jnp.concatenate([bias_ref[h, 2 * typ + mt] for h in range(HEADS_PER_BLOCK)], axis=0)
            mx = jnp.max(s, axis=-1, keepdims=True)
            e = jnp.exp2(s - mx)
            den = jnp.sum(e, axis=-1, keepdims=True)
            oh = jnp.dot(e.astype(BF16), vb, preferred_element_type=F32)
            o = oh[:nq]
            dn = jnp.broadcast_to(den[:nq], (nq, COL_BLOCK))
            for h in range(1, HEADS_PER_BLOCK):
                o = jnp.where(lane_head == h, oh[h * nq:(h + 1) * nq], o)
                dn = jnp.where(lane_head == h, den[h * nq:(h + 1) * nq], dn)
            outs.append(o / dn)
        for i in range(Q_ROWS):
            pieces = []
            for m in range(GRID_W // Q_COLS):
                pieces.append(outs[(m - 1) % 4][Q_COLS * i + Q_ROT:Q_COLS * (i + 1)])
                pieces.append(outs[m][Q_COLS * i:Q_COLS * i + Q_ROT])
            start = pl.multiple_of((Q_ROWS * rb + i) * GRID_W, GRID_W)
            a_ref[0, pl.ds(start, GRID_W), :] = jnp.concatenate(pieces, axis=0).astype(BF16)
        return carry

    lax.fori_loop(0, GRID_H // Q_ROWS, row_block, 0)

    def pool(windows):
        u = u_ref[0].astype(F32)
        pad = jnp.zeros((POOL_HALO, POOL_GROUP_DIM), F32)
        for gi, window in enumerate(windows):
            sl = slice(gi * POOL_GROUP_DIM, (gi + 1) * POOL_GROUP_DIM)
            ug = u[:, sl]
            win = _window_sum(jnp.concatenate([pad, ug, pad], axis=0), window)[POOL_HALO:POOL_HALO + seq]
            pooled = win / cnt_ref[0, :, sl] - ug
            mixed = jnp.dot(pooled.astype(BF16), pw_ref[0, gi].astype(BF16), preferred_element_type=F32) * ps_ref[0, gi]
            p_ref[0, :, sl] = mixed.astype(BF16)

    for cb in range(N_COL_BLOCKS):
        @pl.when(jb == cb)
        def _(cb=cb):
            pool(POOL_WINDOWS[cb * GROUPS_PER_BLOCK:(cb + 1) * GROUPS_PER_BLOCK])


def _mixer(proj, bias, cnt, pool_w, pool_scale, layer, batch, seq):
    proj3 = proj.reshape(batch, seq, D_IN)
    col = lambda base: (lambda j, b: (b, 0, base + j))
    blk = (1, seq, COL_BLOCK)
    bias_blk = (HEADS_PER_BLOCK,) + bias.shape[1:]
    a, p = pl.pallas_call(
        _mix_kernel,
        grid=(N_COL_BLOCKS, batch),
        in_specs=[
            pl.BlockSpec(blk, col(0)),
            pl.BlockSpec(blk, col(N_COL_BLOCKS)),
            pl.BlockSpec(blk, col(2 * N_COL_BLOCKS)),
            pl.BlockSpec(blk, col(3 * N_COL_BLOCKS)),
            pl.BlockSpec(bias_blk, lambda j, b: (layer * N_COL_BLOCKS + j, 0, 0, 0)),
            pl.BlockSpec(blk, lambda j, b: (j, 0, 0)),
            pl.BlockSpec((1, GROUPS_PER_BLOCK, POOL_GROUP_DIM, POOL_GROUP_DIM), lambda j, b: (layer, j, 0, 0)),
            pl.BlockSpec((1, GROUPS_PER_BLOCK, 1, POOL_GROUP_DIM), lambda j, b: (layer, j, 0, 0)),
        ],
        out_specs=[pl.BlockSpec(blk, lambda j, b: (b, 0, j)), pl.BlockSpec(blk, lambda j, b: (b, 0, j))],
        out_shape=[jax.ShapeDtypeStruct((batch, seq, D_ATTN), BF16), jax.ShapeDtypeStruct((batch, seq, D_POOL), BF16)],
        compiler_params=_params(2),
        name="mixer",
    )(proj3, proj3, proj3, proj3, bias, cnt, pool_w, pool_scale)
    return a.reshape(batch * seq, D_ATTN), p.reshape(batch * seq, D_POOL)


ROUTER_ROWS = 32
OUT_SPLIT = 4


def _first_max(vals):
    best = vals[0]
    for v in vals[1:]:
        best = jnp.maximum(best, v)
    idx = jnp.float32(len(vals) - 1)
    for i in range(len(vals) - 2, -1, -1):
        idx = jnp.where(vals[i] == best, jnp.float32(i), idx)
    return best, idx


def _route(lg, norm_scale):
    n = lg.shape[1]
    big = jnp.float32(-3e38)
    g = [lg[i:i + 1] for i in range(N_GROUPS)]
    gmax, gidx = _first_max(g)
    gsum = jnp.exp(g[0] - gmax)
    for v in g[1:]:
        gsum = gsum + jnp.exp(v - gmax)
    gw = 1.0 / gsum
    e = []
    for j in range(EXPERTS_PER_GROUP):
        v = lg[N_GROUPS + j:N_GROUPS + j + 1]
        for gg in range(1, N_GROUPS):
            row = N_GROUPS + EXPERTS_PER_GROUP * gg + j
            v = jnp.where(gidx == gg, lg[row:row + 1], v)
        e.append(v)
    v1, i1 = _first_max(e)
    v2, i2 = _first_max([jnp.where(i1 == j, big, e[j]) for j in range(EXPERTS_PER_GROUP)])
    d = jnp.exp(v2 - v1)
    w1 = gw * (1.0 / (1.0 + d))
    w2 = gw * (d / (1.0 + d))
    first_lo = i1 < i2
    lo = jnp.where(first_lo, i1, i2)
    hi = jnp.where(first_lo, i2, i1)
    w_lo = jnp.where(first_lo, w1, w2)
    w_hi = jnp.where(first_lo, w2, w1)
    cls = 6.0 * gidx + 0.5 * (lo * (7.0 - lo)) + (hi - lo - 1.0)
    return jnp.concatenate([w_lo, w_hi, cls, norm_scale, jnp.zeros((SUBLANES - 4, n), F32)], axis=0)


def _out_kernel(x_tiled, a_ref, p_ref, x_ref, wa_ref, wp_ref, g_ref, wr_ref, x2_ref, r_ref, wa_bf, wp_bf):
    @pl.when(pl.program_id(0) == 0)
    def _():
        wa_bf[...] = wa_ref[0].astype(BF16)
        wp_bf[...] = wp_ref[0].astype(BF16)

    tm = a_ref.shape[0]
    n = tm // OUT_SPLIT
    nt_dims = (((1,), (1,)), ((), ()))
    for c in range(OUT_SPLIT):
        sl = pl.ds(c * n, n)
        tiles = pl.ds(c * n * TOKEN_ROWS, n * TOKEN_ROWS)
        mix = jnp.dot(a_ref[sl, :], wa_bf[...], preferred_element_type=F32)
        mix = mix + jnp.dot(p_ref[sl, :], wp_bf[...], preferred_element_type=F32)
        x2 = (_load_token_tiles(x_ref.at[tiles], n) if x_tiled else x_ref[sl, :]) + mix
        _store_token_tiles(x2_ref.at[tiles], x2)
        scale = _rms_scale(x2)
        scale_row = jnp.transpose(jnp.broadcast_to(scale, (n, LANES)))[:1]
        h = (x2 * scale) * g_ref[0]
        hh = h.astype(BF16)
        hl = (h - hh.astype(F32)).astype(BF16)
        r1 = lax.dot_general(wr_ref[0], hh, nt_dims, preferred_element_type=F32)
        r2 = lax.dot_general(wr_ref[0, :ROUTER_ROWS], hl, nt_dims, preferred_element_type=F32)
        r_ref[0, :, sl] = _route(r1[:ROUTER_ROWS] + r1[ROUTER_ROWS:] + r2, scale_row)


def _out_proj(a, p, x, x_tiled, w_out, g, wr, layer):
    t = a.shape[0]
    tm = TM_OUT
    row = lambda i: (i, 0)
    fixed = lambda i: (layer, 0, 0)
    return pl.pallas_call(
        functools.partial(_out_kernel, x_tiled),
        grid=(t // tm,),
        in_specs=[
            pl.BlockSpec((tm, D_ATTN), row),
            pl.BlockSpec((tm, D_POOL), row),
            _x_spec(tm, x_tiled),
            pl.BlockSpec((1, D_ATTN, D_MODEL), fixed),
            pl.BlockSpec((1, D_POOL, D_MODEL), lambda i: (layer, 1, 0)),
            pl.BlockSpec((1, 1, D_MODEL), fixed),
            pl.BlockSpec((1, 2 * ROUTER_ROWS, D_MODEL), fixed),
        ],
        out_specs=[_x_spec(tm, True),
                   pl.BlockSpec((1, SUBLANES, tm), lambda i: (i, 0, 0))],
        out_shape=[jax.ShapeDtypeStruct((t * TOKEN_ROWS, LANES), F32),
                   jax.ShapeDtypeStruct((t // tm, SUBLANES, tm), F32)],
        scratch_shapes=[pltpu.VMEM((D_ATTN, D_MODEL), BF16), pltpu.VMEM((D_POOL, D_MODEL), BF16)],
        compiler_params=_params(),
        name="out_proj",
    )(a, p, x, w_out, w_out, g, wr)


def _moe_kernel(tok_ref, tile_ref, first_ref, last_ref, lo_ref, hi_ref, newc_ref, elo_ref, ehi_ref,
                h_hbm, w_ref, g_ref, wg_lo, wu_lo, wd_lo, wg_hi, wu_hi, wd_hi, y_hbm,
                xbuf, ybuf, wg_bf, wu_bf, wd_bf, gsem, ssem):
    del elo_ref, ehi_ref
    k = pl.program_id(0)
    tm = xbuf.shape[1] // TOKEN_ROWS
    n_tiles = y_hbm.shape[0] // (tm * TOKEN_ROWS)
    tile = tile_ref[k]
    slot = tile % 2

    @pl.when(newc_ref[k] == 1)
    def _():
        for e, (wg, wu, wd) in enumerate(((wg_lo, wu_lo, wd_lo), (wg_hi, wu_hi, wd_hi))):
            wg_bf[e] = wg[0].astype(BF16)
            wu_bf[e] = wu[0].astype(BF16)
            wd_bf[e] = wd[0].astype(BF16)

    def token_rows(idx):
        return pl.ds(pl.multiple_of(idx * TOKEN_ROWS, TOKEN_ROWS), TOKEN_ROWS)

    def gather_copy(j, tok, sl):
        return pltpu.make_async_copy(h_hbm.at[token_rows(tok)], xbuf.at[sl, token_rows(j)], gsem.at[sl])

    def scatter_copy(j, tok, sl):
        return pltpu.make_async_copy(ybuf.at[sl, token_rows(j)], y_hbm.at[token_rows(tok)], ssem.at[sl])

    def start_rows(copy, tl, sl):
        def body(j, c):
            copy(j, tok_ref[tl * tm + j], sl).start()
            return c
        lax.fori_loop(0, tm, body, 0, unroll=16)

    def wait_rows(copy, sl):
        for j in range(tm):
            copy(j, 0, sl).wait()

    @pl.when(first_ref[k] == 1)
    def _():
        @pl.when(k == 0)
        def _():
            start_rows(gather_copy, tile, slot)
        wait_rows(gather_copy, slot)

        @pl.when(tile + 1 < n_tiles)
        def _():
            start_rows(gather_copy, tile + 1, 1 - slot)

        @pl.when(tile >= 2)
        def _():
            wait_rows(scatter_copy, slot)

    lo = lo_ref[k]
    hi = hi_ref[k]

    @pl.when(hi > lo)
    def _():
        xs = xbuf.at[slot]
        ys = ybuf.at[slot]
        w = w_ref[...]
        x = ((_load_token_tiles(xs, tm) * w[:, 2:3]) * g_ref[0]).astype(BF16)
        row = lax.broadcasted_iota(I32, (tm, 1), 0)
        mine = (row >= lo) & (row < hi)
        w_lo = jnp.where(mine, w[:, 0:1], 0.0)
        w_hi = jnp.where(mine, w[:, 1:2], 0.0)

        def expert(e):
            a = jax.nn.silu(jnp.dot(x, wg_bf[e], preferred_element_type=F32)) * jnp.dot(x, wu_bf[e], preferred_element_type=F32)
            return jnp.dot(a.astype(BF16), wd_bf[e], preferred_element_type=F32)

        y = w_lo * expert(0) + w_hi * expert(1)

        @pl.when(lo == 0)
        def _():
            for s in range(TOKEN_ROWS):
                rows_s = pl.ds(s, tm, stride=TOKEN_ROWS)
                ys[rows_s, :] = xs[rows_s, :] + y[:, s * LANES:(s + 1) * LANES]

        @pl.when(lo > 0)
        def _():
            for s in range(TOKEN_ROWS):
                rows_s = pl.ds(s, tm, stride=TOKEN_ROWS)
                ys[rows_s, :] = ys[rows_s, :] + y[:, s * LANES:(s + 1) * LANES]

    @pl.when(last_ref[k] == 1)
    def _():
        start_rows(scatter_copy, tile, slot)

    @pl.when(k == pl.num_programs(0) - 1)
    def _():
        wait_rows(scatter_copy, slot)
        if n_tiles > 1:
            wait_rows(scatter_copy, 1 - slot)


def _moe_tables(routing, n_tok, tm, expert0):
    w_lo = routing[:, 0, :].reshape(n_tok)
    w_hi = routing[:, 1, :].reshape(n_tok)
    cls = routing[:, 2, :].reshape(n_tok).astype(I32)
    norm_scale = routing[:, 3, :].reshape(n_tok)
    key = cls * n_tok + jnp.arange(n_tok, dtype=I32)
    skey, w_lo, w_hi, norm_scale = lax.sort((key, w_lo, w_hi, norm_scale), num_keys=1, is_stable=False)
    tok_sorted = skey % n_tok
    bounds = jnp.arange(N_CLASSES + 1, dtype=I32) * n_tok
    cstart = jnp.sum((skey[None, :] < bounds[:, None]).astype(I32), axis=1)
    n_tiles = n_tok // tm
    cuts = jnp.sort(jnp.concatenate([jnp.arange(n_tiles, dtype=I32) * tm, cstart[1:N_CLASSES]]))
    ends = jnp.concatenate([cuts[1:], jnp.full((1,), n_tok, I32)])
    tile = jnp.minimum(cuts // tm, n_tiles - 1)
    c = jnp.clip(jnp.sum((cstart[None, :] <= cuts[:, None]).astype(I32), axis=1) - 1, 0, N_CLASSES - 1)
    lo = cuts - tile * tm
    hi = ends - tile * tm
    change = (tile[1:] != tile[:-1]).astype(I32)
    one = jnp.ones((1,), I32)
    first = jnp.concatenate([one, change])
    last = jnp.concatenate([change, one])
    new_class = jnp.concatenate([one, (c[1:] != c[:-1]).astype(I32)])
    grp = c // 6
    e_lo = expert0 + EXPERTS_PER_GROUP * grp + jnp.asarray(PAIR_LO)[c % 6]
    e_hi = expert0 + EXPERTS_PER_GROUP * grp + jnp.asarray(PAIR_HI)[c % 6]
    scalars = tuple(v.astype(I32) for v in (tok_sorted, tile, first, last, lo, hi, new_class, e_lo, e_hi))
    return scalars, jnp.stack([w_lo, w_hi, norm_scale], axis=1)


def _moe(x_tiles, routing, g, w_gate, w_up, w_down, layer):
    n_tok = x_tiles.shape[0] // TOKEN_ROWS
    tm = TM_MOE
    scalars, w_pair = _moe_tables(routing, n_tok, tm, layer * N_GROUPS * EXPERTS_PER_GROUP)
    n_steps = scalars[1].shape[0]
    by_tile = lambda k, tok, tile, first, last, lo, hi, newc, elo, ehi: (tile[k], 0)
    e_lo = lambda k, tok, tile, first, last, lo, hi, newc, elo, ehi: (elo[k], 0, 0)
    e_hi = lambda k, tok, tile, first, last, lo, hi, newc, elo, ehi: (ehi[k], 0, 0)
    up_blk = (1, D_MODEL, D_EXPERT)
    down_blk = (1, D_EXPERT, D_MODEL)
    grid_spec = pltpu.PrefetchScalarGridSpec(
        num_scalar_prefetch=len(scalars),
        grid=(n_steps,),
        in_specs=[
            pl.BlockSpec(memory_space=pl.ANY),
            pl.BlockSpec((tm, 3), by_tile),
            pl.BlockSpec((1, 1, D_MODEL), lambda k, *_: (layer, 0, 0)),
            pl.BlockSpec(up_blk, e_lo), pl.BlockSpec(up_blk, e_lo), pl.BlockSpec(down_blk, e_lo),
            pl.BlockSpec(up_blk, e_hi), pl.BlockSpec(up_blk, e_hi), pl.BlockSpec(down_blk, e_hi),
        ],
        out_specs=pl.BlockSpec(memory_space=pl.ANY),
        scratch_shapes=[
            pltpu.VMEM((2, tm * TOKEN_ROWS, LANES), F32),
            pltpu.VMEM((2, tm * TOKEN_ROWS, LANES), F32),
            pltpu.VMEM((2, D_MODEL, D_EXPERT), BF16),
            pltpu.VMEM((2, D_MODEL, D_EXPERT), BF16),
            pltpu.VMEM((2, D_EXPERT, D_MODEL), BF16),
            pltpu.SemaphoreType.DMA((2,)),
            pltpu.SemaphoreType.DMA((2,)),
        ],
    )
    return pl.pallas_call(
        _moe_kernel,
        grid_spec=grid_spec,
        out_shape=jax.ShapeDtypeStruct((n_tok * TOKEN_ROWS, LANES), F32),
        compiler_params=_params(),
        name="moe",
    )(*scalars, x_tiles, w_pair, g, w_gate, w_up, w_down, w_gate, w_up, w_down)


def _final_kernel(x_ref, g_ref, o_ref):
    x = _load_token_tiles(x_ref, o_ref.shape[0])
    o_ref[...] = (x * _rms_scale(x)) * g_ref[...]


def _final_norm(x_tiles, g):
    t = x_tiles.shape[0] // TOKEN_ROWS
    tm = TM_OUT
    return pl.pallas_call(
        _final_kernel,
        grid=(t // tm,),
        in_specs=[_x_spec(tm, True), pl.BlockSpec((1, D_MODEL), lambda i: (0, 0))],
        out_specs=_x_spec(tm, False),
        out_shape=jax.ShapeDtypeStruct((t, D_MODEL), F32),
        compiler_params=_params(),
        name="final_norm",
    )(x_tiles, g)


def _split_bf16(w):
    hi = w.astype(BF16)
    lo = (w - hi.astype(F32)).astype(BF16)
    return hi, lo


def kernel(x, norm_mix_g, w_in, rpb, pool_w, pool_scale, w_out, norm_ffn_g,
           w_router_group, w_router_expert, w_gate, w_up, w_down, final_g):
    batch, seq, d = x.shape
    depth = w_in.shape[0]
    assert d == D_MODEL and seq == GRID_H * GRID_W
    n_tok = batch * seq
    assert n_tok % TM_DENSE == 0 and n_tok % TM_OUT == 0 and n_tok % TM_MOE == 0

    bias = _attention_bias(rpb)
    cnt = jnp.asarray(_pool_counts(seq))
    n_pool = D_POOL // POOL_GROUP_DIM
    pool_scale4 = pool_scale.reshape(depth, n_pool, 1, POOL_GROUP_DIM)
    g_mix = norm_mix_g.reshape(depth, 1, D_MODEL)
    g_ffn = norm_ffn_g.reshape(depth, 1, D_MODEL)
    n_exp = N_GROUPS * EXPERTS_PER_GROUP
    w_gate_s = w_gate.reshape(depth * n_exp, D_MODEL, D_EXPERT)
    w_up_s = w_up.reshape(depth * n_exp, D_MODEL, D_EXPERT)
    w_down_s = w_down.reshape(depth * n_exp, D_EXPERT, D_MODEL)
    w_router = jnp.concatenate([w_router_group, w_router_expert], axis=-1).transpose(0, 2, 1)
    w_router = jnp.pad(w_router, ((0, 0), (0, ROUTER_ROWS - w_router.shape[1]), (0, 0)))
    w_router_b = jnp.concatenate(_split_bf16(w_router), axis=1)

    xs = x.reshape(n_tok, D_MODEL)
    for l in range(depth):
        tiled = l > 0
        proj = _in_proj(xs, tiled, g_mix, w_in, l, n_tok)
        a, p = _mixer(proj, bias, cnt, pool_w, pool_scale4, l, batch, seq)
        x_mid, routing = _out_proj(a, p, xs, tiled, w_out, g_ffn, w_router_b, l)
        xs = _moe(x_mid, routing, g_ffn, w_gate_s, w_up_s, w_down_s, l)
    out = _final_norm(xs, final_g.reshape(1, D_MODEL))
    return out.reshape(batch, seq, D_MODEL)
```

```python
import functools

import numpy as np
import jax
import jax.numpy as jnp
from jax import lax
from jax.experimental import pallas as pl
from jax.experimental.pallas import tpu as pltpu

F32 = jnp.float32
BF16 = jnp.bfloat16
I32 = jnp.int32

D_MODEL = 1024
GRID_W = 64
GRID_H = 32
D_ATTN = 512
HEAD_DIM = 64
N_HEADS = 8
WIN_H = 8
WIN_W = 16
D_POOL = 512
POOL_GROUP_DIM = 128
D_IN = 2048
N_GROUPS = 4
EXPERTS_PER_GROUP = 4
D_EXPERT = 512
N_CLASSES = 24
EPS = 1e-6
NEG_INF = -1e30
LOG2_E = 1.4426950408889634

LANES = 128
SUBLANES = 8
TOKEN_ROWS = D_MODEL // LANES
COL_BLOCK = 256
HEADS_PER_BLOCK = COL_BLOCK // HEAD_DIM
GROUPS_PER_BLOCK = COL_BLOCK // POOL_GROUP_DIM
N_COL_BLOCKS = D_ATTN // COL_BLOCK
Q_ROWS = 8
Q_COLS = 16
K_ROWS = 16
K_COLS = 32
Q_ROT = 8
ROW_BLOCK_KEY_START = (0, 4, 12, 16)
POOL_HALO = 16
POOL_WINDOWS = (2, 4, 8, 16)

TM_DENSE = 512
TM_OUT = 1024
TM_MOE = 256
DMA_PRIORITIES = 2
VMEM_LIMIT = 56 * 1024 * 1024

PAIR_LO = np.array([0, 0, 0, 1, 1, 2], np.int32)
PAIR_HI = np.array([1, 2, 3, 2, 3, 3], np.int32)


def _params(n_axes=1):
    return pltpu.CompilerParams(dimension_semantics=("arbitrary",) * n_axes, vmem_limit_bytes=VMEM_LIMIT)


def _rms_scale(x):
    return lax.rsqrt(jnp.mean(x * x, axis=-1, keepdims=True) + EPS)


def _load_token_tiles(ref, n_tok):
    return jnp.concatenate([ref[pl.ds(s, n_tok, stride=TOKEN_ROWS), :] for s in range(TOKEN_ROWS)], axis=1)


def _store_token_tiles(ref, val):
    n_tok = val.shape[0]
    for s in range(TOKEN_ROWS):
        ref[pl.ds(s, n_tok, stride=TOKEN_ROWS), :] = val[:, s * LANES:(s + 1) * LANES]


def _x_spec(tm, tiled):
    shape = (tm * TOKEN_ROWS, LANES) if tiled else (tm, D_MODEL)
    return pl.BlockSpec(shape, lambda i: (i, 0))


def _in_kernel(x_tiled, x_ref, g_ref, w_ref, proj_ref, w_bf):
    @pl.when(pl.program_id(0) == 0)
    def _():
        w_bf[...] = w_ref[0].astype(BF16)

    tm = proj_ref.shape[0]
    x = _load_token_tiles(x_ref, tm) if x_tiled else x_ref[...]
    h = ((x * _rms_scale(x)) * g_ref[0]).astype(BF16)
    proj = jnp.dot(h, w_bf[...], preferred_element_type=F32)
    q = (proj[:, :D_ATTN] * (HEAD_DIM ** -0.5 * LOG2_E)).reshape(tm // GRID_W, GRID_W, D_ATTN)
    q = jnp.concatenate([q[:, Q_ROT:], q[:, :Q_ROT]], axis=1).reshape(tm, D_ATTN)
    proj_ref[:, :D_ATTN] = q.astype(BF16)
    proj_ref[:, D_ATTN:] = proj[:, D_ATTN:].astype(BF16)


def _in_proj(x, x_tiled, g, w, layer, n_tok):
    tm = TM_DENSE
    fixed = lambda i: (layer, 0, 0)
    return pl.pallas_call(
        functools.partial(_in_kernel, x_tiled),
        grid=(n_tok // tm,),
        in_specs=[_x_spec(tm, x_tiled), pl.BlockSpec((1, 1, D_MODEL), fixed), pl.BlockSpec((1, D_MODEL, D_IN), fixed)],
        out_specs=pl.BlockSpec((tm, D_IN), lambda i: (i, 0)),
        out_shape=jax.ShapeDtypeStruct((n_tok, D_IN), BF16),
        scratch_shapes=[pltpu.VMEM((D_MODEL, D_IN), BF16)],
        compiler_params=_params(),
        name="in_proj",
    )(x, g, w)


def _key_col_segments(m):
    if m < 3:
        return ((Q_COLS * m, K_COLS),)
    return ((0, K_COLS // 2), (GRID_W - K_COLS // 2, K_COLS // 2))


def _key_col(m, kk):
    for start, n in _key_col_segments(m):
        if kk < n:
            return start + kk
        kk -= n
    raise ValueError(kk)


def _attention_index_tables():
    dr = np.zeros((3, Q_ROWS, K_ROWS), np.int32)
    rv = np.zeros((3, Q_ROWS, K_ROWS), bool)
    for typ, rb in enumerate((0, 1, 3)):
        for i in range(Q_ROWS):
            r = Q_ROWS * rb + i
            r_start = min(max(r - WIN_H // 2, 0), GRID_H - WIN_H)
            for kr in range(K_ROWS):
                ka = ROW_BLOCK_KEY_START[rb] + kr
                rv[typ, i, kr] = r_start <= ka < r_start + WIN_H
                dr[typ, i, kr] = min(max(ka - r + WIN_H - 1, 0), 2 * WIN_H - 2)
    dc = np.zeros((2, Q_COLS, K_COLS), np.int32)
    cv = np.zeros((2, Q_COLS, K_COLS), bool)
    for mt, m in enumerate((0, 3)):
        for j in range(Q_COLS):
            qc = (Q_COLS * m + j + Q_ROT) % GRID_W
            c_start = min(max(qc - WIN_W // 2, 0), GRID_W - WIN_W)
            for kk in range(K_COLS):
                kc = _key_col(m, kk)
                cv[mt, j, kk] = c_start <= kc < c_start + WIN_W
                dc[mt, j, kk] = min(max(kc - qc, -(WIN_W - 1)), WIN_W - 1) + WIN_W - 1
    return dr, rv, dc, cv


KEY_ROWS_PER_LANE_TILE = LANES // K_COLS


def _bias_kernel(row_index, e_ref, o_ref):
    for typ in range(row_index.shape[0]):
        for mt in range(2):
            for i in range(Q_ROWS):
                for c in range(K_ROWS // KEY_ROWS_PER_LANE_TILE):
                    tile = None
                    for s in range(KEY_ROWS_PER_LANE_TILE):
                        piece = e_ref[0, int(row_index[typ, i, KEY_ROWS_PER_LANE_TILE * c + s]), mt, s]
                        tile = piece if tile is None else tile + piece
                    o_ref[0, 2 * typ + mt, pl.ds(Q_COLS * i, Q_COLS), pl.ds(LANES * c, LANES)] = tile


def _attention_bias(rpb):
    dr, rv, dc, cv = _attention_index_tables()
    nl, nh, n_dr, _ = rpb.shape
    e = jnp.where(jnp.asarray(cv)[None, None, None], rpb[:, :, :, dc].astype(F32) * LOG2_E, NEG_INF)
    e = jnp.concatenate([e, jnp.full((nl, nh, 1) + e.shape[3:], NEG_INF, F32)], axis=2)
    slots = [jnp.pad(e, ((0, 0),) * 5 + ((K_COLS * s, LANES - K_COLS * (s + 1)),))
             for s in range(KEY_ROWS_PER_LANE_TILE)]
    e4 = jnp.stack(slots, axis=4).reshape(nl * nh, n_dr + 1, 2, KEY_ROWS_PER_LANE_TILE, Q_COLS, LANES)
    row_index = np.where(rv, dr, n_dr)
    nq, nk = Q_ROWS * Q_COLS, K_ROWS * K_COLS
    return pl.pallas_call(
        functools.partial(_bias_kernel, row_index),
        grid=(nl * nh,),
        in_specs=[pl.BlockSpec((1,) + e4.shape[1:], lambda i: (i, 0, 0, 0, 0, 0))],
        out_specs=pl.BlockSpec((1, 6, nq, nk), lambda i: (i, 0, 0, 0)),
        out_shape=jax.ShapeDtypeStruct((nl * nh, 6, nq, nk), F32),
        compiler_params=_params(),
        name="attention_bias",
    )(e4)


def _pool_counts(seq):
    t = np.arange(seq)
    cols = []
    for w in POOL_WINDOWS:
        cnt = np.minimum(t + w // 2, seq) - np.maximum(t - w // 2, 0)
        cols.append(np.broadcast_to(cnt[:, None].astype(np.float32), (seq, POOL_GROUP_DIM)))
    return np.concatenate(cols, axis=1).reshape(seq, N_COL_BLOCKS, COL_BLOCK).transpose(1, 0, 2)


def _window_sum(ue, window):
    n = ue.shape[0]

    def shifted(x, d):
        return pltpu.roll(x, d % n, axis=0)

    s = ue + shifted(ue, 1)
    reach = 1
    while 2 * reach < window:
        s = shifted(s, reach) + shifted(s, -reach)
        reach *= 2
    return s


def _mix_kernel(q_ref, k_ref, v_ref, u_ref, bias_ref, cnt_ref, pw_ref, ps_ref, a_ref, p_ref):
    jb = pl.program_id(0)
    seq = q_ref.shape[1]
    nq = Q_ROWS * Q_COLS
    nk = K_ROWS * K_COLS
    lane_head = lax.broadcasted_iota(I32, (nq, COL_BLOCK), 1) // HEAD_DIM

    def rows(ref, grid_row, col, n):
        return ref[0, pl.ds(pl.multiple_of(grid_row * GRID_W + col, Q_COLS), n), :]

    def row_block(rb, carry):
        key_row0 = jnp.clip(Q_ROWS * rb - WIN_H // 2, 0, GRID_H - K_ROWS)
        typ = jnp.where(rb == 0, 0, jnp.where(rb == GRID_H // Q_ROWS - 1, 2, 1))
        outs = []
        for m in range(GRID_W // Q_COLS):
            mt = 0 if m < 3 else 1
            qb = jnp.concatenate([rows(q_ref, Q_ROWS * rb + i, Q_COLS * m, Q_COLS) for i in range(Q_ROWS)], axis=0)
            segs = _key_col_segments(m)
            kb = jnp.concatenate(
                [rows(k_ref, key_row0 + kr, c0, n) for kr in range(K_ROWS) for c0, n in segs], axis=0)
            vb = jnp.concatenate(
                [rows(v_ref, key_row0 + kr, c0, n) for kr in range(K_ROWS) for c0, n in segs], axis=0)
            zero = jnp.zeros_like(qb)
            qh = jnp.concatenate([jnp.where(lane_head == h, qb, zero) for h in range(HEADS_PER_BLOCK)], axis=0)
            s = lax.dot_general(qh, kb, (((1,), (1,)), ((), ())), preferred_element_type=F32)
            s = s + jnp.concatenate([bias_ref[h, 2 * typ + mt] for h in range(HEADS_PER_BLOCK)], axis=0)
            mx = jnp.max(s, axis=-1, keepdims=True)
            e = jnp.exp2(s - mx)
            den = jnp.sum(e, axis=-1, keepdims=True)
            oh = jnp.dot(e.astype(BF16), vb, preferred_element_type=F32)
            o = oh[:nq]
            dn = jnp.broadcast_to(den[:nq], (nq, COL_BLOCK))
            for h in range(1, HEADS_PER_BLOCK):
                o = jnp.where(lane_head == h, oh[h * nq:(h + 1) * nq], o)
                dn = jnp.where(lane_head == h, den[h * nq:(h + 1) * nq], dn)
            outs.append(o / dn)
        for i in range(Q_ROWS):
            pieces = []
            for m in range(GRID_W // Q_COLS):
                pieces.append(outs[(m - 1) % 4][Q_COLS * i + Q_ROT:Q_COLS * (i + 1)])
                pieces.append(outs[m][Q_COLS * i:Q_COLS * i + Q_ROT])
            start = pl.multiple_of((Q_ROWS * rb + i) * GRID_W, GRID_W)
            a_ref[0, pl.ds(start, GRID_W), :] = jnp.concatenate(pieces, axis=0).astype(BF16)
        return carry

    lax.fori_loop(0, GRID_H // Q_ROWS, row_block, 0)

    def pool(windows):
        u = u_ref[0].astype(F32)
        pad = jnp.zeros((POOL_HALO, POOL_GROUP_DIM), F32)
        for gi, window in enumerate(windows):
            sl = slice(gi * POOL_GROUP_DIM, (gi + 1) * POOL_GROUP_DIM)
            ug = u[:, sl]
            win = _window_sum(jnp.concatenate([pad, ug, pad], axis=0), window)[POOL_HALO:POOL_HALO + seq]
            pooled = win / cnt_ref[0, :, sl] - ug
            mixed = jnp.dot(pooled.astype(BF16), pw_ref[0, gi].astype(BF16), preferred_element_type=F32) * ps_ref[0, gi]
            p_ref[0, :, sl] = mixed.astype(BF16)

    for cb in range(N_COL_BLOCKS):
        @pl.when(jb == cb)
        def _(cb=cb):
            pool(POOL_WINDOWS[cb * GROUPS_PER_BLOCK:(cb + 1) * GROUPS_PER_BLOCK])


def _mixer(proj, bias, cnt, pool_w, pool_scale, layer, batch, seq):
    proj3 = proj.reshape(batch, seq, D_IN)
    col = lambda base: (lambda j, b: (b, 0, base + j))
    blk = (1, seq, COL_BLOCK)
    bias_blk = (HEADS_PER_BLOCK,) + bias.shape[1:]
    a, p = pl.pallas_call(
        _mix_kernel,
        grid=(N_COL_BLOCKS, batch),
        in_specs=[
            pl.BlockSpec(blk, col(0)),
            pl.BlockSpec(blk, col(N_COL_BLOCKS)),
            pl.BlockSpec(blk, col(2 * N_COL_BLOCKS)),
            pl.BlockSpec(blk, col(3 * N_COL_BLOCKS)),
            pl.BlockSpec(bias_blk, lambda j, b: (layer * N_COL_BLOCKS + j, 0, 0, 0)),
            pl.BlockSpec(blk, lambda j, b: (j, 0, 0)),
            pl.BlockSpec((1, GROUPS_PER_BLOCK, POOL_GROUP_DIM, POOL_GROUP_DIM), lambda j, b: (layer, j, 0, 0)),
            pl.BlockSpec((1, GROUPS_PER_BLOCK, 1, POOL_GROUP_DIM), lambda j, b: (layer, j, 0, 0)),
        ],
        out_specs=[pl.BlockSpec(blk, lambda j, b: (b, 0, j)), pl.BlockSpec(blk, lambda j, b: (b, 0, j))],
        out_shape=[jax.ShapeDtypeStruct((batch, seq, D_ATTN), BF16), jax.ShapeDtypeStruct((batch, seq, D_POOL), BF16)],
        compiler_params=_params(2),
        name="mixer",
    )(proj3, proj3, proj3, proj3, bias, cnt, pool_w, pool_scale)
    return a.reshape(batch * seq, D_ATTN), p.reshape(batch * seq, D_POOL)


ROUTER_ROWS = 32
OUT_SPLIT = 4


def _first_max(vals):
    best = vals[0]
    for v in vals[1:]:
        best = jnp.maximum(best, v)
    idx = jnp.float32(len(vals) - 1)
    for i in range(len(vals) - 2, -1, -1):
        idx = jnp.where(vals[i] == best, jnp.float32(i), idx)
    return best, idx


def _route(lg, norm_scale):
    n = lg.shape[1]
    big = jnp.float32(-3e38)
    g = [lg[i:i + 1] for i in range(N_GROUPS)]
    gmax, gidx = _first_max(g)
    gsum = jnp.exp(g[0] - gmax)
    for v in g[1:]:
        gsum = gsum + jnp.exp(v - gmax)
    gw = 1.0 / gsum
    e = []
    for j in range(EXPERTS_PER_GROUP):
        v = lg[N_GROUPS + j:N_GROUPS + j + 1]
        for gg in range(1, N_GROUPS):
            row = N_GROUPS + EXPERTS_PER_GROUP * gg + j
            v = jnp.where(gidx == gg, lg[row:row + 1], v)
        e.append(v)
    v1, i1 = _first_max(e)
    v2, i2 = _first_max([jnp.where(i1 == j, big, e[j]) for j in range(EXPERTS_PER_GROUP)])
    d = jnp.exp(v2 - v1)
    w1 = gw * (1.0 / (1.0 + d))
    w2 = gw * (d / (1.0 + d))
    first_lo = i1 < i2
    lo = jnp.where(first_lo, i1, i2)
    hi = jnp.where(first_lo, i2, i1)
    w_lo = jnp.where(first_lo, w1, w2)
    w_hi = jnp.where(first_lo, w2, w1)
    cls = 6.0 * gidx + 0.5 * (lo * (7.0 - lo)) + (hi - lo - 1.0)
    return jnp.concatenate([w_lo, w_hi, cls, norm_scale, jnp.zeros((SUBLANES - 4, n), F32)], axis=0)


def _out_kernel(x_tiled, a_ref, p_ref, x_ref, wa_ref, wp_ref, g_ref, wr_ref, x2_ref, r_ref, wa_bf, wp_bf):
    @pl.when(pl.program_id(0) == 0)
    def _():
        wa_bf[...] = wa_ref[0].astype(BF16)
        wp_bf[...] = wp_ref[0].astype(BF16)

    tm = a_ref.shape[0]
    n = tm // OUT_SPLIT
    nt_dims = (((1,), (1,)), ((), ()))
    for c in range(OUT_SPLIT):
        sl = pl.ds(c * n, n)
        tiles = pl.ds(c * n * TOKEN_ROWS, n * TOKEN_ROWS)
        mix = jnp.dot(a_ref[sl, :], wa_bf[...], preferred_element_type=F32)
        mix = mix + jnp.dot(p_ref[sl, :], wp_bf[...], preferred_element_type=F32)
        x2 = (_load_token_tiles(x_ref.at[tiles], n) if x_tiled else x_ref[sl, :]) + mix
        _store_token_tiles(x2_ref.at[tiles], x2)
        scale = _rms_scale(x2)
        scale_row = jnp.transpose(jnp.broadcast_to(scale, (n, LANES)))[:1]
        h = (x2 * scale) * g_ref[0]
        hh = h.astype(BF16)
        hl = (h - hh.astype(F32)).astype(BF16)
        r1 = lax.dot_general(wr_ref[0], hh, nt_dims, preferred_element_type=F32)
        r2 = lax.dot_general(wr_ref[0, :ROUTER_ROWS], hl, nt_dims, preferred_element_type=F32)
        r_ref[0, :, sl] = _route(r1[:ROUTER_ROWS] + r1[ROUTER_ROWS:] + r2, scale_row)


def _out_proj(a, p, x, x_tiled, w_out, g, wr, layer):
    t = a.shape[0]
    tm = TM_OUT
    row = lambda i: (i, 0)
    fixed = lambda i: (layer, 0, 0)
    return pl.pallas_call(
        functools.partial(_out_kernel, x_tiled),
        grid=(t // tm,),
        in_specs=[
            pl.BlockSpec((tm, D_ATTN), row),
            pl.BlockSpec((tm, D_POOL), row),
            _x_spec(tm, x_tiled),
            pl.BlockSpec((1, D_ATTN, D_MODEL), fixed),
            pl.BlockSpec((1, D_POOL, D_MODEL), lambda i: (layer, 1, 0)),
            pl.BlockSpec((1, 1, D_MODEL), fixed),
            pl.BlockSpec((1, 2 * ROUTER_ROWS, D_MODEL), fixed),
        ],
        out_specs=[_x_spec(tm, True),
                   pl.BlockSpec((1, SUBLANES, tm), lambda i: (i, 0, 0))],
        out_shape=[jax.ShapeDtypeStruct((t * TOKEN_ROWS, LANES), F32),
                   jax.ShapeDtypeStruct((t // tm, SUBLANES, tm), F32)],
        scratch_shapes=[pltpu.VMEM((D_ATTN, D_MODEL), BF16), pltpu.VMEM((D_POOL, D_MODEL), BF16)],
        compiler_params=_params(),
        name="out_proj",
    )(a, p, x, w_out, w_out, g, wr)


def _moe_kernel(tok_ref, tile_ref, first_ref, last_ref, lo_ref, hi_ref, newc_ref, elo_ref, ehi_ref,
                h_hbm, w_ref, g_ref, wg_lo, wu_lo, wd_lo, wg_hi, wu_hi, wd_hi, y_hbm,
                xbuf, ybuf, wg_bf, wu_bf, wd_bf, gsem, ssem):
    del elo_ref, ehi_ref
    k = pl.program_id(0)
    tm = xbuf.shape[1] // TOKEN_ROWS
    n_tiles = y_hbm.shape[0] // (tm * TOKEN_ROWS)
    tile = tile_ref[k]
    slot = tile % 2

    @pl.when(newc_ref[k] == 1)
    def _():
        for e, (wg, wu, wd) in enumerate(((wg_lo, wu_lo, wd_lo), (wg_hi, wu_hi, wd_hi))):
            wg_bf[e] = wg[0].astype(BF16)
            wu_bf[e] = wu[0].astype(BF16)
            wd_bf[e] = wd[0].astype(BF16)

    def token_rows(idx):
        return pl.ds(pl.multiple_of(idx * TOKEN_ROWS, TOKEN_ROWS), TOKEN_ROWS)

    def gather_copy(j, tok, sl):
        return pltpu.make_async_copy(h_hbm.at[token_rows(tok)], xbuf.at[sl, token_rows(j)], gsem.at[sl])

    def scatter_copy(j, tok, sl):
        return pltpu.make_async_copy(ybuf.at[sl, token_rows(j)], y_hbm.at[token_rows(tok)], ssem.at[sl])

    def start_rows(copy, tl, sl):
        def body(jj, c):
            for p in range(DMA_PRIORITIES):
                j = DMA_PRIORITIES * jj + p
                copy(j, tok_ref[tl * tm + j], sl).start(priority=p)
            return c
        lax.fori_loop(0, tm // DMA_PRIORITIES, body, 0, unroll=8)

    def wait_rows(copy, sl):
        for j in range(tm):
            copy(j, 0, sl).wait()

    @pl.when(first_ref[k] == 1)
    def _():
        @pl.when(k == 0)
        def _():
            start_rows(gather_copy, tile, slot)
        wait_rows(gather_copy, slot)

        @pl.when(tile + 1 < n_tiles)
        def _():
            start_rows(gather_copy, tile + 1, 1 - slot)

        @pl.when(tile >= 2)
        def _():
            wait_rows(scatter_copy, slot)

    lo = lo_ref[k]
    hi = hi_ref[k]

    @pl.when(hi > lo)
    def _():
        xs = xbuf.at[slot]
        ys = ybuf.at[slot]
        w = w_ref[...]
        x = ((_load_token_tiles(xs, tm) * w[:, 2:3]) * g_ref[0]).astype(BF16)
        row = lax.broadcasted_iota(I32, (tm, 1), 0)
        mine = (row >= lo) & (row < hi)
        w_lo = jnp.where(mine, w[:, 0:1], 0.0)
        w_hi = jnp.where(mine, w[:, 1:2], 0.0)

        def expert(e):
            a = jax.nn.silu(jnp.dot(x, wg_bf[e], preferred_element_type=F32)) * jnp.dot(x, wu_bf[e], preferred_element_type=F32)
            return jnp.dot(a.astype(BF16), wd_bf[e], preferred_element_type=F32)

        y = w_lo * expert(0) + w_hi * expert(1)

        @pl.when(lo == 0)
        def _():
            for s in range(TOKEN_ROWS):
                rows_s = pl.ds(s, tm, stride=TOKEN_ROWS)
                ys[rows_s, :] = xs[rows_s, :] + y[:, s * LANES:(s + 1) * LANES]

        @pl.when(lo > 0)
        def _():
            for s in range(TOKEN_ROWS):
                rows_s = pl.ds(s, tm, stride=TOKEN_ROWS)
                ys[rows_s, :] = ys[rows_s, :] + y[:, s * LANES:(s + 1) * LANES]

    @pl.when(last_ref[k] == 1)
    def _():
        start_rows(scatter_copy, tile, slot)

    @pl.when(k == pl.num_programs(0) - 1)
    def _():
        wait_rows(scatter_copy, slot)
        if n_tiles > 1:
            wait_rows(scatter_copy, 1 - slot)


def _moe_tables(routing, n_tok, tm, expert0):
    w_lo = routing[:, 0, :].reshape(n_tok)
    w_hi = routing[:, 1, :].reshape(n_tok)
    cls = routing[:, 2, :].reshape(n_tok).astype(I32)
    norm_scale = routing[:, 3, :].reshape(n_tok)
    key = cls * n_tok + jnp.arange(n_tok, dtype=I32)
    skey, w_lo, w_hi, norm_scale = lax.sort((key, w_lo, w_hi, norm_scale), num_keys=1, is_stable=False)
    tok_sorted = skey % n_tok
    bounds = jnp.arange(N_CLASSES + 1, dtype=I32) * n_tok
    cstart = jnp.sum((skey[None, :] < bounds[:, None]).astype(I32), axis=1)
    n_tiles = n_tok // tm
    cuts = jnp.sort(jnp.concatenate([jnp.arange(n_tiles, dtype=I32) * tm, cstart[1:N_CLASSES]]))
    ends = jnp.concatenate([cuts[1:], jnp.full((1,), n_tok, I32)])
    tile = jnp.minimum(cuts // tm, n_tiles - 1)
    c = jnp.clip(jnp.sum((cstart[None, :] <= cuts[:, None]).astype(I32), axis=1) - 1, 0, N_CLASSES - 1)
    lo = cuts - tile * tm
    hi = ends - tile * tm
    change = (tile[1:] != tile[:-1]).astype(I32)
    one = jnp.ones((1,), I32)
    first = jnp.concatenate([one, change])
    last = jnp.concatenate([change, one])
    new_class = jnp.concatenate([one, (c[1:] != c[:-1]).astype(I32)])
    grp = c // 6
    e_lo = expert0 + EXPERTS_PER_GROUP * grp + jnp.asarray(PAIR_LO)[c % 6]
    e_hi = expert0 + EXPERTS_PER_GROUP * grp + jnp.asarray(PAIR_HI)[c % 6]
    scalars = tuple(v.astype(I32) for v in (tok_sorted, tile, first, last, lo, hi, new_class, e_lo, e_hi))
    return scalars, jnp.stack([w_lo, w_hi, norm_scale], axis=1)


def _moe(x_tiles, routing, g, w_gate, w_up, w_down, layer):
    n_tok = x_tiles.shape[0] // TOKEN_ROWS
    tm = TM_MOE
    scalars, w_pair = _moe_tables(routing, n_tok, tm, layer * N_GROUPS * EXPERTS_PER_GROUP)
    n_steps = scalars[1].shape[0]
    by_tile = lambda k, tok, tile, first, last, lo, hi, newc, elo, ehi: (tile[k], 0)
    e_lo = lambda k, tok, tile, first, last, lo, hi, newc, elo, ehi: (elo[k], 0, 0)
    e_hi = lambda k, tok, tile, first, last, lo, hi, newc, elo, ehi: (ehi[k], 0, 0)
    up_blk = (1, D_MODEL, D_EXPERT)
    down_blk = (1, D_EXPERT, D_MODEL)
    grid_spec = pltpu.PrefetchScalarGridSpec(
        num_scalar_prefetch=len(scalars),
        grid=(n_steps,),
        in_specs=[
            pl.BlockSpec(memory_space=pl.ANY),
            pl.BlockSpec((tm, 3), by_tile),
            pl.BlockSpec((1, 1, D_MODEL), lambda k, *_: (layer, 0, 0)),
            pl.BlockSpec(up_blk, e_lo), pl.BlockSpec(up_blk, e_lo), pl.BlockSpec(down_blk, e_lo),
            pl.BlockSpec(up_blk, e_hi), pl.BlockSpec(up_blk, e_hi), pl.BlockSpec(down_blk, e_hi),
        ],
        out_specs=pl.BlockSpec(memory_space=pl.ANY),
        scratch_shapes=[
            pltpu.VMEM((2, tm * TOKEN_ROWS, LANES), F32),
            pltpu.VMEM((2, tm * TOKEN_ROWS, LANES), F32),
            pltpu.VMEM((2, D_MODEL, D_EXPERT), BF16),
            pltpu.VMEM((2, D_MODEL, D_EXPERT), BF16),
            pltpu.VMEM((2, D_EXPERT, D_MODEL), BF16),
            pltpu.SemaphoreType.DMA((2,)),
            pltpu.SemaphoreType.DMA((2,)),
        ],
    )
    return pl.pallas_call(
        _moe_kernel,
        grid_spec=grid_spec,
        out_shape=jax.ShapeDtypeStruct((n_tok * TOKEN_ROWS, LANES), F32),
        compiler_params=_params(),
        name="moe",
    )(*scalars, x_tiles, w_pair, g, w_gate, w_up, w_down, w_gate, w_up, w_down)


def _final_kernel(x_ref, g_ref, o_ref):
    x = _load_token_tiles(x_ref, o_ref.shape[0])
    o_ref[...] = (x * _rms_scale(x)) * g_ref[...]


def _final_norm(x_tiles, g):
    t = x_tiles.shape[0] // TOKEN_ROWS
    tm = TM_OUT
    return pl.pallas_call(
        _final_kernel,
        grid=(t // tm,),
        in_specs=[_x_spec(tm, True), pl.BlockSpec((1, D_MODEL), lambda i: (0, 0))],
        out_specs=_x_spec(tm, False),
        out_shape=jax.ShapeDtypeStruct((t, D_MODEL), F32),
        compiler_params=_params(),
        name="final_norm",
    )(x_tiles, g)


def _split_bf16(w):
    hi = w.astype(BF16)
    lo = (w - hi.astype(F32)).astype(BF16)
    return hi, lo


def kernel(x, norm_mix_g, w_in, rpb, pool_w, pool_scale, w_out, norm_ffn_g,
           w_router_group, w_router_expert, w_gate, w_up, w_down, final_g):
    batch, seq, d = x.shape
    depth = w_in.shape[0]
    assert d == D_MODEL and seq == GRID_H * GRID_W
    n_tok = batch * seq
    assert n_tok % TM_DENSE == 0 and n_tok % TM_OUT == 0 and n_tok % TM_MOE == 0

    bias = _attention_bias(rpb)
    cnt = jnp.asarray(_pool_counts(seq))
    n_pool = D_POOL // POOL_GROUP_DIM
    pool_scale4 = pool_scale.reshape(depth, n_pool, 1, POOL_GROUP_DIM)
    g_mix = norm_mix_g.reshape(depth, 1, D_MODEL)
    g_ffn = norm_ffn_g.reshape(depth, 1, D_MODEL)
    n_exp = N_GROUPS * EXPERTS_PER_GROUP
    w_gate_s = w_gate.reshape(depth * n_exp, D_MODEL, D_EXPERT)
    w_up_s = w_up.reshape(depth * n_exp, D_MODEL, D_EXPERT)
    w_down_s = w_down.reshape(depth * n_exp, D_EXPERT, D_MODEL)
    w_router = jnp.concatenate([w_router_group, w_router_expert], axis=-1).transpose(0, 2, 1)
    w_router = jnp.pad(w_router, ((0, 0), (0, ROUTER_ROWS - w_router.shape[1]), (0, 0)))
    w_router_b = jnp.concatenate(_split_bf16(w_router), axis=1)

    xs = x.reshape(n_tok, D_MODEL)
    for l in range(depth):
        tiled = l > 0
        proj = _in_proj(xs, tiled, g_mix, w_in, l, n_tok)
        a, p = _mixer(proj, bias, cnt, pool_w, pool_scale4, l, batch, seq)
        x_mid, routing = _out_proj(a, p, xs, tiled, w_out, g_ffn, w_router_b, l)
        xs = _moe(x_mid, routing, g_ffn, w_gate_s, w_up_s, w_down_s, l)
    out = _final_norm(xs, final_g.reshape(1, D_MODEL))
    return out.reshape(batch, seq, D_MODEL)
```
